```python
import math
import jax, jax.numpy as jnp
from jax import lax
import numpy as np


D_MODEL = 1024
BATCH = 16
SEQ = 2048
DEPTH = 1
DEC_BATCH = 8
DEC_SEQ = 8192
PAST_LEN = 128

RET_HEADS = 4
RET_DK = 256
RET_DV = 256
RET_QK_W = RET_HEADS * RET_DK
RET_V_W = RET_HEADS * RET_DV
RET_CHUNK = 128
ROPE_BASE = 10000.0
POOL_WINDOWS = (2, 4, 8, 16)
POOL_GROUPS = 4
POOL_GW = 256
POOL_W = POOL_GROUPS * POOL_GW
MEM_TOKENS = 256
MEM_HEADS = 4
MEM_HD = 256
MEM_W = MEM_HEADS * MEM_HD
N_BRANCH = 3
D_FF = 2816
CONV_W = 3
EPS = 1e-6

IN_SPLITS = (RET_QK_W, 2 * RET_QK_W, 2 * RET_QK_W + RET_V_W, 2 * RET_QK_W + 2 * RET_V_W,
             2 * RET_QK_W + 2 * RET_V_W + POOL_W, 2 * RET_QK_W + 2 * RET_V_W + POOL_W + MEM_W)
IN_COLS = 2 * RET_QK_W + 2 * RET_V_W + POOL_W + MEM_W + N_BRANCH * D_MODEL

kernel_name = "hybrid_retention_pool_memory_encoder"


def rmsnorm(x, g):
    xf = x.astype(jnp.float32)
    y = xf * lax.rsqrt(jnp.mean(xf * xf, axis=-1, keepdims=True) + EPS)
    return (y * g.astype(jnp.float32)).astype(x.dtype)


def rotary(x):
    s, d = x.shape[1], x.shape[-1]
    half = d // 2
    inv = ROPE_BASE ** (-jnp.arange(half, dtype=jnp.float32) / half)
    ang = jnp.arange(s, dtype=jnp.float32)[:, None] * inv[None, :]
    cos = jnp.cos(ang)[None, :, None, :]
    sin = jnp.sin(ang)[None, :, None, :]
    xf = x.astype(jnp.float32)
    x1, x2 = xf[..., :half], xf[..., half:]
    return jnp.concatenate([x1 * cos - x2 * sin, x1 * sin + x2 * cos], axis=-1)


def retention_one_direction(q, k, v, log_gamma, inclusive):
    b, h, s, dk = q.shape
    dv = v.shape[-1]
    c = RET_CHUNK
    n = s // c
    qc = q.reshape(b, h, n, c, dk)
    kc = k.reshape(b, h, n, c, dk)
    vc = v.reshape(b, h, n, c, dv)
    idx = jnp.arange(c, dtype=jnp.float32)
    diff = idx[:, None] - idx[None, :]
    mask = diff >= 0 if inclusive else diff > 0
    decay = jnp.where(mask[None], jnp.exp(log_gamma[:, None, None] * jnp.maximum(diff, 0.0)[None]), 0.0)
    scores = jnp.einsum('bhncd,bhnmd->bhncm', qc, kc) * decay[None, :, None]
    y_intra = jnp.einsum('bhncm,bhnme->bhnce', scores, vc)
    q_dec = jnp.exp(log_gamma[:, None] * (idx + 1.0))[None, :, :, None]
    k_dec = jnp.exp(log_gamma[:, None] * (c - 1.0 - idx))[None, :, :, None]
    chunk_dec = jnp.exp(log_gamma * c)[None, :, None, None]

    def step(state, inp):
        q_i, k_i, v_i = inp
        y = jnp.einsum('bhcd,bhde->bhce', q_i, state) * q_dec
        state = state * chunk_dec + jnp.einsum('bhcd,bhce->bhde', k_i * k_dec, v_i)
        return state, y

    init = jnp.zeros((b, h, dk, dv), jnp.float32)
    xs = (jnp.moveaxis(qc, 2, 0), jnp.moveaxis(kc, 2, 0), jnp.moveaxis(vc, 2, 0))
    _, y_cross = lax.scan(step, init, xs)
    y = y_intra + jnp.moveaxis(y_cross, 0, 2)
    return y.reshape(b, h, s, dv)


def retention_branch(rq, rk, rv, rg, decay_fwd, decay_bwd, ret_gn):
    b, s, _ = rq.shape
    q = rotary(rq.reshape(b, s, RET_HEADS, RET_DK)).transpose(0, 2, 1, 3)
    k = (rotary(rk.reshape(b, s, RET_HEADS, RET_DK)) * (RET_DK ** -0.5)).transpose(0, 2, 1, 3)
    v = rv.astype(jnp.float32).reshape(b, s, RET_HEADS, RET_DV).transpose(0, 2, 1, 3)
    lg_f = jax.nn.log_sigmoid(decay_fwd.astype(jnp.float32))
    lg_b = jax.nn.log_sigmoid(decay_bwd.astype(jnp.float32))
    y_f = retention_one_direction(q, k, v, lg_f, True)
    y_b = jnp.flip(retention_one_direction(jnp.flip(q, 2), jnp.flip(k, 2), jnp.flip(v, 2), lg_b, False), 2)
    y = y_f + y_b
    mu = jnp.mean(y, axis=-1, keepdims=True)
    var = jnp.mean(jnp.square(y - mu), axis=-1, keepdims=True)
    y = (y - mu) * lax.rsqrt(var + EPS)
    y = y.transpose(0, 2, 1, 3).reshape(b, s, RET_V_W) * ret_gn.astype(jnp.float32)
    y = jax.nn.silu(rg.astype(jnp.float32)) * y
    return y.astype(rq.dtype)


def pool_branch(p, pool_w, pool_scale):
    b, s, _ = p.shape
    pf = p.astype(jnp.float32)
    cs = jnp.concatenate([jnp.zeros((b, 1, POOL_W), jnp.float32), jnp.cumsum(pf, axis=1)], axis=1)
    pos = jnp.arange(s)
    outs = []
    for gi, w in enumerate(POOL_WINDOWS):
        lo = jnp.maximum(pos - w // 2, 0)
        hi = jnp.minimum(pos + w // 2, s)
        sl = slice(gi * POOL_GW, (gi + 1) * POOL_GW)
        csg = cs[:, :, sl]
        mean = (csg[:, hi] - csg[:, lo]) / (hi - lo).astype(jnp.float32)[None, :, None]
        outs.append(mean - pf[:, :, sl])
    d = jnp.stack(outs, axis=2)
    y = jnp.einsum('bsgc,gcd->bsgd', d, pool_w.astype(jnp.float32)).reshape(b, s, POOL_W)
    return (y * pool_scale.astype(jnp.float32)).astype(p.dtype)


def memory_branch(mq, mem, g_mem, w_mem_kv):
    b, s, _ = mq.shape
    m = mem.shape[1]
    kv = rmsnorm(mem, g_mem) @ w_mem_kv
    k, v = jnp.split(kv, 2, axis=-1)
    k = k.reshape(b, m, MEM_HEADS, MEM_HD)
    v = v.reshape(b, m, MEM_HEADS, MEM_HD)
    q = mq.reshape(b, s, MEM_HEADS, MEM_HD)
    scores = jnp.einsum('bshd,bmhd->bhsm', q, k).astype(jnp.float32) * (MEM_HD ** -0.5)
    probs = jax.nn.softmax(scores, axis=-1).astype(v.dtype)
    out = jnp.einsum('bhsm,bmhd->bshd', probs, v)
    return out.reshape(b, s, MEM_W)


def conv_ffn(h, w_up, conv_w, conv_b, w_down):
    u = h @ w_up
    u = lax.conv_general_dilated(u, conv_w[:, None, :].astype(u.dtype), (1,), ((CONV_W // 2, CONV_W // 2),),
                                 dimension_numbers=('NWC', 'WIO', 'NWC'),
                                 feature_group_count=2 * D_FF) + conv_b
    gate, val = jnp.split(u, 2, axis=-1)
    return (jax.nn.gelu(gate, approximate=True) * val) @ w_down


def encoder_layer(x, mem, g_mix_pre, g_mix_post, g_mem, w_in, decay_fwd, decay_bwd, ret_gn, w_ret_out,
                  pool_w, pool_scale, w_pool_out, w_mem_kv, w_mem_out, w_o,
                  g_ffn_pre, g_ffn_post, w_up, conv_w, conv_b, w_down):
    b, s, _ = x.shape
    h = rmsnorm(x, g_mix_pre)
    z = h @ w_in
    rq, rk, rv, rg, pin, mq, gl = jnp.split(z, list(IN_SPLITS), axis=-1)
    gates = jax.nn.sigmoid(gl.astype(jnp.float32)).reshape(b, s, N_BRANCH, D_MODEL).astype(x.dtype)
    y_ret = retention_branch(rq, rk, rv, rg, decay_fwd, decay_bwd, ret_gn) @ w_ret_out
    y_pool = pool_branch(pin, pool_w, pool_scale) @ w_pool_out
    y_mem = memory_branch(mq, mem, g_mem, w_mem_kv) @ w_mem_out
    merged = gates[:, :, 0] * y_ret + gates[:, :, 1] * y_pool + gates[:, :, 2] * y_mem
    x = x + rmsnorm(merged @ w_o, g_mix_post)
    h2 = rmsnorm(x, g_ffn_pre)
    x = x + rmsnorm(conv_ffn(h2, w_up, conv_w, conv_b, w_down), g_ffn_post)
    return x


def trunk(x, mem, weights):
    for l in range(DEPTH):
        x = encoder_layer(x, mem, *[w[l] for w in weights])
    return x


def setup_inputs(seed: int = 0) -> dict:
    key = jax.random.key(seed)
    ks = jax.random.split(key, 26)
    f32 = jnp.float32

    def nrm(k, shape, scale):
        return jax.random.normal(k, shape, f32) * scale

    def gain(k, shape):
        return 1.0 + 0.05 * jax.random.normal(k, shape, f32)

    decay_base = jnp.log(2.0 ** (5.0 + jnp.arange(RET_HEADS, dtype=f32)) - 1.0)
    L = DEPTH
    return {
        "x_prompt": nrm(ks[0], (BATCH, SEQ, D_MODEL), 1.0),
        "x_sample": nrm(ks[1], (DEC_BATCH, DEC_SEQ, D_MODEL), 1.0),
        "mem_prompt": nrm(ks[2], (BATCH, MEM_TOKENS, D_MODEL), 1.0),
        "mem_sample": nrm(ks[3], (DEC_BATCH, MEM_TOKENS, D_MODEL), 1.0),
        "g_mix_pre": gain(ks[4], (L, D_MODEL)),
        "g_mix_post": gain(ks[5], (L, D_MODEL)),
        "g_mem": gain(ks[6], (L, D_MODEL)),
        "w_in": nrm(ks[7], (L, D_MODEL, IN_COLS), D_MODEL ** -0.5),
        "decay_fwd": decay_base[None, :] + 0.1 * jax.random.normal(ks[8], (L, RET_HEADS), f32),
        "decay_bwd": decay_base[None, :] + 0.1 * jax.random.normal(ks[9], (L, RET_HEADS), f32),
        "ret_gn": gain(ks[10], (L, RET_V_W)),
        "w_ret_out": nrm(ks[11], (L, RET_V_W, D_MODEL), RET_V_W ** -0.5),
        "pool_w": nrm(ks[12], (L, POOL_GROUPS, POOL_GW, POOL_GW), POOL_GW ** -0.5),
        "pool_scale": gain(ks[13], (L, POOL_W)),
        "w_pool_out": nrm(ks[14], (L, POOL_W, D_MODEL), POOL_W ** -0.5),
        "w_mem_kv": nrm(ks[15], (L, D_MODEL, 2 * MEM_W), D_MODEL ** -0.5),
        "w_mem_out": nrm(ks[16], (L, MEM_W, D_MODEL), MEM_W ** -0.5),
        "w_o": nrm(ks[17], (L, D_MODEL, D_MODEL), D_MODEL ** -0.5),
        "g_ffn_pre": gain(ks[18], (L, D_MODEL)),
        "g_ffn_post": gain(ks[19], (L, D_MODEL)),
        "w_up": nrm(ks[20], (L, D_MODEL, 2 * D_FF), D_MODEL ** -0.5),
        "conv_w": nrm(ks[21], (L, CONV_W, 2 * D_FF), CONV_W ** -0.5),
        "conv_b": nrm(ks[22], (L, 2 * D_FF), 0.02),
        "w_down": nrm(ks[23], (L, D_FF, D_MODEL), D_FF ** -0.5),
    }


def reference(x_prompt, x_sample, mem_prompt, mem_sample, g_mix_pre, g_mix_post, g_mem, w_in,
              decay_fwd, decay_bwd, ret_gn, w_ret_out, pool_w, pool_scale, w_pool_out,
              w_mem_kv, w_mem_out, w_o, g_ffn_pre, g_ffn_post, w_up, conv_w, conv_b, w_down):
    weights = (g_mix_pre, g_mix_post, g_mem, w_in, decay_fwd, decay_bwd, ret_gn, w_ret_out,
               pool_w, pool_scale, w_pool_out, w_mem_kv, w_mem_out, w_o,
               g_ffn_pre, g_ffn_post, w_up, conv_w, conv_b, w_down)
    y_prompt = trunk(x_prompt, mem_prompt, weights)
    y_sample = trunk(x_sample, mem_sample, weights)
    return (y_prompt, y_sample)
```

```python
import functools

import jax
import jax.numpy as jnp
from jax import lax
from jax.experimental import pallas as pl
from jax.experimental.pallas import tpu as pltpu

F32 = jnp.float32
BF16 = jnp.bfloat16

D_MODEL = 1024
N_HEADS = 4
HEAD_W = 256
HALF_W = HEAD_W // 2
RET_CHUNK = 128
ROPE_BASE = 10000.0
POOL_WINDOWS = (2, 4, 8, 16)
MEM_TOKENS = 256
N_BRANCH = 3
D_FF = 2816
EPS = 1e-6
QK_SCALE = HEAD_W ** -0.5

VMEM_LIMIT_BYTES = 56 * 1024 * 1024
BF16_SUBLANES = 16
F32_SUBLANES = 8

ROW_TILE = 512
RET_TILE = 512
FF_CHUNK = 256


def _rms(x, g):
    ms = jnp.mean(x * x, axis=-1, keepdims=True)
    return x * lax.rsqrt(ms + EPS) * g


def _sigmoid(x):
    return 1.0 / (1.0 + jnp.exp(-x))


def _const_spec(shape):
    zeros = (0,) * len(shape)
    return pl.BlockSpec(shape, lambda *_: zeros, pipeline_mode=pl.Buffered(1))


def _mem_kv_kernel(m_ref, g_ref, w_ref, o_ref):
    h = _rms(m_ref[...], g_ref[...]).astype(BF16)
    o_ref[...] = jnp.dot(h, w_ref[...], preferred_element_type=F32).astype(BF16)


def _mem_kv(mem2d, g_mem, w_kv):
    rows = mem2d.shape[0]
    tm = MEM_TOKENS
    return pl.pallas_call(
        _mem_kv_kernel,
        out_shape=jax.ShapeDtypeStruct((rows, 2 * D_MODEL), BF16),
        grid=(rows // tm,),
        in_specs=[
            pl.BlockSpec((tm, D_MODEL), lambda i: (i, 0)),
            _const_spec((1, D_MODEL)),
            _const_spec((D_MODEL, 2 * D_MODEL)),
        ],
        out_specs=pl.BlockSpec((tm, 2 * D_MODEL), lambda i: (i, 0)),
        compiler_params=pltpu.CompilerParams(
            dimension_semantics=("parallel",), vmem_limit_bytes=VMEM_LIMIT_BYTES),
        name="mem_kv",
    )(mem2d, g_mem, w_kv)


def _in_proj_kernel(x_ref, g_ref, w_ref, cos_ref, sin_ref,
                    q_ref, k_ref, v_ref, sg_ref, p_ref, mq_ref, gt_ref):
    h = _rms(x_ref[...], g_ref[...]).astype(BF16)
    cos = cos_ref[...]
    sin = sin_ref[...]

    def seg(i):
        return jnp.dot(h, w_ref[:, i * D_MODEL:(i + 1) * D_MODEL], preferred_element_type=F32)

    def rotary(z, out_ref, scale):
        for hd in range(N_HEADS):
            lo = hd * HEAD_W
            x1 = z[:, lo:lo + HALF_W]
            x2 = z[:, lo + HALF_W:lo + HEAD_W]
            out_ref[:, lo:lo + HALF_W] = ((x1 * cos - x2 * sin) * scale).astype(BF16)
            out_ref[:, lo + HALF_W:lo + HEAD_W] = ((x1 * sin + x2 * cos) * scale).astype(BF16)

    rotary(seg(0), q_ref, 1.0)
    rotary(seg(1), k_ref, QK_SCALE)
    v_ref[...] = seg(2).astype(BF16)
    rg = seg(3)
    sg_ref[...] = (rg * _sigmoid(rg)).astype(BF16)
    p_ref[...] = seg(4).astype(BF16)
    mq_ref[...] = seg(5).astype(BF16)
    for b in range(N_BRANCH):
        gt_ref[:, b * D_MODEL:(b + 1) * D_MODEL] = _sigmoid(seg(6 + b)).astype(BF16)


def _in_proj(x2d, g, w_in, cos, sin, seq):
    rows = x2d.shape[0]
    tm = ROW_TILE
    tiles_per_seq = seq // tm
    row_spec = pl.BlockSpec((tm, D_MODEL), lambda i: (i, 0))
    tab_spec = pl.BlockSpec((tm, HALF_W), lambda i: (i % tiles_per_seq, 0))
    out1 = jax.ShapeDtypeStruct((rows, D_MODEL), BF16)
    return pl.pallas_call(
        _in_proj_kernel,
        out_shape=(out1,) * 6 + (jax.ShapeDtypeStruct((rows, N_BRANCH * D_MODEL), BF16),),
        grid=(rows // tm,),
        in_specs=[row_spec, _const_spec((1, D_MODEL)), _const_spec(w_in.shape), tab_spec, tab_spec],
        out_specs=(row_spec,) * 6 + (pl.BlockSpec((tm, N_BRANCH * D_MODEL), lambda i: (i, 0)),),
        compiler_params=pltpu.CompilerParams(
            dimension_semantics=("parallel",), vmem_limit_bytes=VMEM_LIMIT_BYTES),
        name="in_proj",
    )(x2d, g, w_in, cos, sin)


def _log_sigmoid(x):
    return -(jnp.maximum(-x, 0.0) + jnp.log1p(jnp.exp(-jnp.abs(x))))


def _retention_kernel(dec_ref, q_ref, k_ref, v_ref, sg_ref, gn_ref, o_ref,
                      ycross_ref, state_ref, dmask_ref, qdec_ref, kdec_ref, cdec_ref, *, n_tiles):
    c = RET_CHUNK
    hd = pl.program_id(1)
    ps = pl.program_id(2)
    t = pl.program_id(3)
    n_chunks = RET_TILE // c

    @pl.when(t == 0)
    def _init_pass():
        state_ref[...] = jnp.zeros_like(state_ref)
        lg = _log_sigmoid(jnp.full((c, HEAD_W), dec_ref[ps, hd], F32))
        idx = lax.broadcasted_iota(jnp.int32, (c, HEAD_W), 0).astype(F32)
        fwd = ps == 1
        qdec_ref[...] = jnp.exp(lg * jnp.where(fwd, idx + 1.0, c - idx))
        kdec_ref[...] = jnp.exp(lg * jnp.where(fwd, c - 1.0 - idx, idx))
        cdec_ref[...] = jnp.exp(_log_sigmoid(jnp.full((F32_SUBLANES, HEAD_W), dec_ref[ps, hd], F32)) * c)

    @pl.when(jnp.logical_and(t == 0, ps == 1))
    def _init_mask():
        lg_f = _log_sigmoid(jnp.full((c, c), dec_ref[1, hd], F32))
        lg_b = _log_sigmoid(jnp.full((c, c), dec_ref[0, hd], F32))
        diff = (lax.broadcasted_iota(jnp.int32, (c, c), 0)
                - lax.broadcasted_iota(jnp.int32, (c, c), 1)).astype(F32)
        dmask_ref[...] = jnp.where(diff >= 0, jnp.exp(lg_f * jnp.maximum(diff, 0.0)),
                                   jnp.exp(lg_b * jnp.maximum(-diff, 0.0)))

    def update_state(k, v):
        kd = (k.astype(F32) * kdec_ref[...]).astype(BF16)
        kv = lax.dot_general(kd, v, (((0,), (0,)), ((), ())), preferred_element_type=F32)
        state_ref[...] = state_ref[...] * cdec_ref[0:1, :] + kv

    def cross(q):
        return jnp.dot(q, state_ref[...].astype(BF16), preferred_element_type=F32) * qdec_ref[...]

    @pl.when(ps == 0)
    def _backward_cross():
        tile_row = (n_tiles - 1 - t) * RET_TILE
        for ci in reversed(range(n_chunks)):
            rows = slice(ci * c, (ci + 1) * c)
            dst = pl.ds(pl.multiple_of(tile_row + ci * c, c), c)
            ycross_ref[dst, :] = cross(q_ref[0, rows, :])
            update_state(k_ref[0, rows, :], v_ref[0, rows, :])

    @pl.when(ps == 1)
    def _forward_finish():
        tile_row = t * RET_TILE
        for ci in range(n_chunks):
            rows = slice(ci * c, (ci + 1) * c)
            src = pl.ds(pl.multiple_of(tile_row + ci * c, c), c)
            q = q_ref[0, rows, :]
            k = k_ref[0, rows, :]
            v = v_ref[0, rows, :]
            s = lax.dot_general(q, k, (((1,), (1,)), ((), ())), preferred_element_type=F32)
            p = (s * dmask_ref[...]).astype(BF16)
            y = jnp.dot(p, v, preferred_element_type=F32) + cross(q) + ycross_ref[src, :]
            update_state(k, v)
            mu = jnp.mean(y, axis=-1, keepdims=True)
            yc = y - mu
            var = jnp.mean(yc * yc, axis=-1, keepdims=True)
            yn = yc * lax.rsqrt(var + EPS) * gn_ref[...]
            o_ref[0, rows, :] = (sg_ref[0, rows, :].astype(F32) * yn).astype(BF16)


def _retention(dec, q, k, v, sg, ret_gn):
    batch, seq, _ = q.shape
    n_tiles = seq // RET_TILE

    def tile_idx(ps, t):
        return jnp.where(ps == 0, n_tiles - 1 - t, t)

    in_spec = pl.BlockSpec((1, RET_TILE, HEAD_W), lambda b, h, ps, t: (b, tile_idx(ps, t), h))
    out_spec = pl.BlockSpec((1, RET_TILE, HEAD_W), lambda b, h, ps, t: (b, ps * t, h))
    return pl.pallas_call(
        functools.partial(_retention_kernel, n_tiles=n_tiles),
        out_shape=jax.ShapeDtypeStruct(q.shape, BF16),
        grid=(batch, N_HEADS, 2, n_tiles),
        in_specs=[
            pl.BlockSpec(memory_space=pltpu.SMEM),
            in_spec, in_spec, in_spec, in_spec,
            pl.BlockSpec((1, HEAD_W), lambda b, h, ps, t: (0, h)),
        ],
        out_specs=out_spec,
        scratch_shapes=[
            pltpu.VMEM((seq, HEAD_W), F32),
            pltpu.VMEM((HEAD_W, HEAD_W), F32),
            pltpu.VMEM((RET_CHUNK, RET_CHUNK), F32),
            pltpu.VMEM((RET_CHUNK, HEAD_W), F32),
            pltpu.VMEM((RET_CHUNK, HEAD_W), F32),
            pltpu.VMEM((F32_SUBLANES, HEAD_W), F32),
        ],
        compiler_params=pltpu.CompilerParams(
            dimension_semantics=("arbitrary",) * 4, vmem_limit_bytes=VMEM_LIMIT_BYTES),
        name="retention",
    )(dec, q, k, v, sg, ret_gn)


def _mix_out_kernel(x_ref, yr_ref, p_ref, pprev_ref, pnext_ref, mq_ref, gt_ref, kv_ref,
                    w_ret_ref, pool_w_ref, pool_scale_ref, w_pool_ref, w_mem_ref, w_o_ref, g_ref,
                    o_ref, pext_ref, *, seq):
    tm = ROW_TILE
    halo = BF16_SUBLANES
    t = pl.program_id(1)
    has_prev = (t > 0).astype(F32)
    has_next = (t < pl.num_programs(1) - 1).astype(F32)

    merged = gt_ref[0, :, 0:D_MODEL].astype(F32) * jnp.dot(
        yr_ref[0], w_ret_ref[...], preferred_element_type=F32)

    pext_ref[0:halo, :] = pprev_ref[0].astype(F32) * has_prev
    pext_ref[halo:halo + tm, :] = p_ref[0].astype(F32)
    pext_ref[halo + tm:, :] = pnext_ref[0].astype(F32) * has_next
    pos = t * tm + lax.broadcasted_iota(jnp.int32, (tm, HEAD_W), 0)
    y_pool = None
    for gi, w in enumerate(POOL_WINDOWS):
        cols = slice(gi * HEAD_W, (gi + 1) * HEAD_W)
        acc = pext_ref[halo - w // 2:halo - w // 2 + tm, cols]
        for j in range(1, w):
            acc = acc + pext_ref[halo - w // 2 + j:halo - w // 2 + j + tm, cols]
        cnt = (jnp.minimum(pos + w // 2, seq) - jnp.maximum(pos - w // 2, 0)).astype(F32)
        d = (acc / cnt - pext_ref[halo:halo + tm, cols]).astype(BF16)
        yg = jnp.dot(d, pool_w_ref[gi], preferred_element_type=F32) * pool_scale_ref[:, cols]
        part = jnp.dot(yg.astype(BF16), w_pool_ref[cols, :], preferred_element_type=F32)
        y_pool = part if y_pool is None else y_pool + part
    merged = merged + gt_ref[0, :, D_MODEL:2 * D_MODEL].astype(F32) * y_pool

    y_mem = None
    for hd in range(N_HEADS):
        cols = slice(hd * HEAD_W, (hd + 1) * HEAD_W)
        kh = kv_ref[0, :, hd * HEAD_W:(hd + 1) * HEAD_W]
        vh = kv_ref[0, :, D_MODEL + hd * HEAD_W:D_MODEL + (hd + 1) * HEAD_W]
        s = lax.dot_general(mq_ref[0, :, cols], kh, (((1,), (1,)), ((), ())),
                            preferred_element_type=F32) * QK_SCALE
        e = jnp.exp(s - jnp.max(s, axis=-1, keepdims=True))
        probs = (e / jnp.sum(e, axis=-1, keepdims=True)).astype(BF16)
        oh = jnp.dot(probs, vh, preferred_element_type=F32).astype(BF16)
        part = jnp.dot(oh, w_mem_ref[cols, :], preferred_element_type=F32)
        y_mem = part if y_mem is None else y_mem + part
    merged = merged + gt_ref[0, :, 2 * D_MODEL:].astype(F32) * y_mem

    y = jnp.dot(merged.astype(BF16), w_o_ref[...], preferred_element_type=F32)
    o_ref[0] = x_ref[0] + _rms(y, g_ref[...])


def _mix_out(x, yr, p, mq, gates, kv, w_ret, pool_w, pool_scale, w_pool, w_mem, w_o, g_post):
    batch, seq, _ = x.shape
    tm = ROW_TILE
    halo = BF16_SUBLANES
    n_tiles = seq // tm
    halo_per_tile = tm // halo
    n_halo_blocks = seq // halo
    row_spec = pl.BlockSpec((1, tm, D_MODEL), lambda b, t: (b, t, 0))
    prev_spec = pl.BlockSpec((1, halo, D_MODEL),
                             lambda b, t: (b, jnp.maximum(t * halo_per_tile - 1, 0), 0))
    next_spec = pl.BlockSpec((1, halo, D_MODEL),
                             lambda b, t: (b, jnp.minimum((t + 1) * halo_per_tile, n_halo_blocks - 1), 0))
    return pl.pallas_call(
        functools.partial(_mix_out_kernel, seq=seq),
        out_shape=jax.ShapeDtypeStruct(x.shape, F32),
        grid=(batch, n_tiles),
        in_specs=[
            row_spec, row_spec, row_spec, prev_spec, next_spec, row_spec,
            pl.BlockSpec((1, tm, N_BRANCH * D_MODEL), lambda b, t: (b, t, 0)),
            pl.BlockSpec((1, MEM_TOKENS, 2 * D_MODEL), lambda b, t: (b, 0, 0)),
            _const_spec(w_ret.shape), _const_spec(pool_w.shape), _const_spec(pool_scale.shape),
            _const_spec(w_pool.shape), _const_spec(w_mem.shape), _const_spec(w_o.shape),
            _const_spec(g_post.shape),
        ],
        out_specs=row_spec,
        scratch_shapes=[pltpu.VMEM((tm + 2 * halo, D_MODEL), F32)],
        compiler_params=pltpu.CompilerParams(
            dimension_semantics=("parallel", "parallel"), vmem_limit_bytes=VMEM_LIMIT_BYTES),
        name="mix_out",
    )(x, yr, p, p, p, mq, gates, kv, w_ret, pool_w, pool_scale, w_pool, w_mem, w_o, g_post)


def _gelu_tanh(x):
    return 0.5 * x * (1.0 + jnp.tanh(0.7978845608028654 * (x + 0.044715 * (x * x * x))))


def _ffn_kernel(x_ref, xprev_ref, xnext_ref, g_pre_ref, w_up_ref, conv_w_ref, conv_b_ref,
                w_down_ref, g_post_ref, o_ref, u_ref):
    tm = ROW_TILE
    halo = F32_SUBLANES
    t = pl.program_id(1)
    has_prev = (t > 0).astype(F32)
    has_next = (t < pl.num_programs(1) - 1).astype(F32)
    x = x_ref[0]
    xext = jnp.concatenate([xprev_ref[0] * has_prev, x, xnext_ref[0] * has_next], axis=0)
    hext = _rms(xext, g_pre_ref[...]).astype(BF16)

    def conv(col0):
        cols = slice(col0, col0 + FF_CHUNK)
        u_ref[...] = jnp.dot(hext, w_up_ref[:, cols], preferred_element_type=F32)
        return (u_ref[halo - 1:halo - 1 + tm, :] * conv_w_ref[0:1, cols]
                + u_ref[halo:halo + tm, :] * conv_w_ref[1:2, cols]
                + u_ref[halo + 1:halo + 1 + tm, :] * conv_w_ref[2:3, cols]
                + conv_b_ref[:, cols])

    acc = None
    for j in range(D_FF // FF_CHUNK):
        gate = conv(j * FF_CHUNK)
        val = conv(D_FF + j * FF_CHUNK)
        a = (_gelu_tanh(gate) * val).astype(BF16)
        part = jnp.dot(a, w_down_ref[j * FF_CHUNK:(j + 1) * FF_CHUNK, :], preferred_element_type=F32)
        acc = part if acc is None else acc + part
    o_ref[0] = x + _rms(acc, g_post_ref[...])


def _ffn(x, g_pre, w_up, conv_w, conv_b, w_down, g_post):
    batch, seq, _ = x.shape
    tm = ROW_TILE
    halo = F32_SUBLANES
    n_tiles = seq // tm
    halo_per_tile = tm // halo
    n_halo_blocks = seq // halo
    row_spec = pl.BlockSpec((1, tm, D_MODEL), lambda b, t: (b, t, 0))
    prev_spec = pl.BlockSpec((1, halo, D_MODEL),
                             lambda b, t: (b, jnp.maximum(t * halo_per_tile - 1, 0), 0))
    next_spec = pl.BlockSpec((1, halo, D_MODEL),
                             lambda b, t: (b, jnp.minimum((t + 1) * halo_per_tile, n_halo_blocks - 1), 0))
    return pl.pallas_call(
        _ffn_kernel,
        out_shape=jax.ShapeDtypeStruct(x.shape, F32),
        grid=(batch, n_tiles),
        in_specs=[
            row_spec, prev_spec, next_spec,
            _const_spec(g_pre.shape), _const_spec(w_up.shape), _const_spec(conv_w.shape),
            _const_spec(conv_b.shape), _const_spec(w_down.shape), _const_spec(g_post.shape),
        ],
        out_specs=row_spec,
        scratch_shapes=[pltpu.VMEM((tm + 2 * halo, FF_CHUNK), F32)],
        compiler_params=pltpu.CompilerParams(
            dimension_semantics=("parallel", "parallel"), vmem_limit_bytes=VMEM_LIMIT_BYTES),
        name="ffn",
    )(x, x, x, g_pre, w_up, conv_w, conv_b, w_down, g_post)


def _rope_tables(seq):
    inv = ROPE_BASE ** (-jnp.arange(HALF_W, dtype=F32) / HALF_W)
    ang = jnp.arange(seq, dtype=F32)[:, None] * inv[None, :]
    return jnp.cos(ang), jnp.sin(ang)


def _layer(x, mem, w):
    batch, seq, _ = x.shape
    cos, sin = _rope_tables(seq)
    kv = _mem_kv(mem.reshape(batch * MEM_TOKENS, D_MODEL), w["g_mem"], w["w_mem_kv"])
    kv = kv.reshape(batch, MEM_TOKENS, 2 * D_MODEL)
    q, k, v, sg, p, mq, gates = _in_proj(
        x.reshape(batch * seq, D_MODEL), w["g_mix_pre"], w["w_in"], cos, sin, seq)
    shp = (batch, seq, D_MODEL)
    yr = _retention(w["decay"], q.reshape(shp), k.reshape(shp), v.reshape(shp), sg.reshape(shp),
                    w["ret_gn"])
    x1 = _mix_out(x, yr, p.reshape(shp), mq.reshape(shp), gates.reshape(batch, seq, N_BRANCH * D_MODEL),
                  kv, w["w_ret_out"], w["pool_w"], w["pool_scale"], w["w_pool_out"], w["w_mem_out"],
                  w["w_o"], w["g_mix_post"])
    return _ffn(x1, w["g_ffn_pre"], w["w_up"], w["conv_w"], w["conv_b"], w["w_down"], w["g_ffn_post"])


def kernel(x_prompt, x_sample, mem_prompt, mem_sample, g_mix_pre, g_mix_post, g_mem, w_in,
           decay_fwd, decay_bwd, ret_gn, w_ret_out, pool_w, pool_scale, w_pool_out,
           w_mem_kv, w_mem_out, w_o, g_ffn_pre, g_ffn_post, w_up, conv_w, conv_b, w_down):
    depth = w_in.shape[0]
    for l in range(depth):
        w = {
            "g_mix_pre": g_mix_pre[l][None], "g_mix_post": g_mix_post[l][None], "g_mem": g_mem[l][None],
            "w_in": w_in[l].astype(BF16),
            "decay": jnp.stack([decay_bwd[l], decay_fwd[l]]).astype(F32),
            "ret_gn": ret_gn[l][None],
            "w_ret_out": w_ret_out[l].astype(BF16), "pool_w": pool_w[l].astype(BF16),
            "pool_scale": pool_scale[l][None], "w_pool_out": w_pool_out[l].astype(BF16),
            "w_mem_kv": w_mem_kv[l].astype(BF16), "w_mem_out": w_mem_out[l].astype(BF16),
            "w_o": w_o[l].astype(BF16),
            "g_ffn_pre": g_ffn_pre[l][None], "g_ffn_post": g_ffn_post[l][None],
            "w_up": w_up[l].astype(BF16), "conv_w": conv_w[l], "conv_b": conv_b[l][None],
            "w_down": w_down[l].astype(BF16),
        }
        x_prompt = _layer(x_prompt, mem_prompt, w)
        x_sample = _layer(x_sample, mem_sample, w)
    return (x_prompt, x_sample)
```

```python
import functools

import jax
import jax.numpy as jnp
from jax import lax
from jax.experimental import pallas as pl
from jax.experimental.pallas import tpu as pltpu

F32 = jnp.float32
BF16 = jnp.bfloat16

D_MODEL = 1024
N_HEADS = 4
HEAD_W = 256
HALF_W = HEAD_W // 2
RET_CHUNK = 512
ROPE_BASE = 10000.0
POOL_WINDOWS = (2, 4, 8, 16)
MEM_TOKENS = 256
N_BRANCH = 3
D_FF = 2816
EPS = 1e-6
QK_SCALE = HEAD_W ** -0.5

VMEM_LIMIT_BYTES = 56 * 1024 * 1024
LANES = 128
BF16_SUBLANES = 16
F32_SUBLANES = 8

ROW_TILE = 512
FF_CHUNK = 256
FF_BUFS = 3


def _rms(x, g):
    ms = jnp.mean(x * x, axis=-1, keepdims=True)
    return x * lax.rsqrt(ms + EPS) * g


def _sigmoid(x):
    return 1.0 / (1.0 + jnp.exp(-x))


def _const_spec(shape):
    zeros = (0,) * len(shape)
    return pl.BlockSpec(shape, lambda *_: zeros, pipeline_mode=pl.Buffered(1))


def _mem_kv_kernel(m_ref, g_ref, w_ref, o_ref):
    h = _rms(m_ref[...], g_ref[...]).astype(BF16)
    o_ref[...] = jnp.dot(h, w_ref[...], preferred_element_type=F32).astype(BF16)


def _mem_kv(mem2d, g_mem, w_kv):
    rows = mem2d.shape[0]
    tm = MEM_TOKENS
    return pl.pallas_call(
        _mem_kv_kernel,
        out_shape=jax.ShapeDtypeStruct((rows, 2 * D_MODEL), BF16),
        grid=(rows // tm,),
        in_specs=[
            pl.BlockSpec((tm, D_MODEL), lambda i: (i, 0)),
            _const_spec((1, D_MODEL)),
            _const_spec((D_MODEL, 2 * D_MODEL)),
        ],
        out_specs=pl.BlockSpec((tm, 2 * D_MODEL), lambda i: (i, 0)),
        compiler_params=pltpu.CompilerParams(
            dimension_semantics=("parallel",), vmem_limit_bytes=VMEM_LIMIT_BYTES),
        name="mem_kv",
    )(mem2d, g_mem, w_kv)


def _in_proj_kernel(x_ref, g_ref, w_ref, cos_ref, sin_ref,
                    q_ref, k_ref, v_ref, sg_ref, p_ref, mq_ref, gt_ref):
    h = _rms(x_ref[...], g_ref[...]).astype(BF16)
    cos = cos_ref[...]
    sin = sin_ref[...]

    def seg(i):
        return jnp.dot(h, w_ref[:, i * D_MODEL:(i + 1) * D_MODEL], preferred_element_type=F32)

    def rotary(z, out_ref, scale):
        for hd in range(N_HEADS):
            lo = hd * HEAD_W
            x1 = z[:, lo:lo + HALF_W]
            x2 = z[:, lo + HALF_W:lo + HEAD_W]
            out_ref[:, lo:lo + HALF_W] = ((x1 * cos - x2 * sin) * scale).astype(BF16)
            out_ref[:, lo + HALF_W:lo + HEAD_W] = ((x1 * sin + x2 * cos) * scale).astype(BF16)

    rotary(seg(0), q_ref, 1.0)
    rotary(seg(1), k_ref, QK_SCALE)
    v_ref[...] = seg(2).astype(BF16)
    rg = seg(3)
    sg_ref[...] = (rg * _sigmoid(rg)).astype(BF16)
    p_ref[...] = seg(4).astype(BF16)
    mq_ref[...] = seg(5).astype(BF16)
    for b in range(N_BRANCH):
        gt_ref[:, b * D_MODEL:(b + 1) * D_MODEL] = _sigmoid(seg(6 + b)).astype(BF16)


def _in_proj(x2d, g, w_in, cos, sin, seq):
    rows = x2d.shape[0]
    tm = ROW_TILE
    tiles_per_seq = seq // tm
    row_spec = pl.BlockSpec((tm, D_MODEL), lambda i: (i, 0))
    tab_spec = pl.BlockSpec((tm, HALF_W), lambda i: (i % tiles_per_seq, 0))
    out1 = jax.ShapeDtypeStruct((rows, D_MODEL), BF16)
    return pl.pallas_call(
        _in_proj_kernel,
        out_shape=(out1,) * 6 + (jax.ShapeDtypeStruct((rows, N_BRANCH * D_MODEL), BF16),),
        grid=(rows // tm,),
        in_specs=[row_spec, _const_spec((1, D_MODEL)), _const_spec(w_in.shape), tab_spec, tab_spec],
        out_specs=(row_spec,) * 6 + (pl.BlockSpec((tm, N_BRANCH * D_MODEL), lambda i: (i, 0)),),
        compiler_params=pltpu.CompilerParams(
            dimension_semantics=("parallel",), vmem_limit_bytes=VMEM_LIMIT_BYTES),
        name="in_proj",
    )(x2d, g, w_in, cos, sin)


def _log_sigmoid(x):
    return -(jnp.maximum(-x, 0.0) + jnp.log1p(jnp.exp(-jnp.abs(x))))


def _head_cols(hd):
    return slice(hd * HEAD_W, (hd + 1) * HEAD_W)


def _ret_state_kernel(dec_ref, k_ref, v_ref, o_ref, state_ref, kdec_ref, cdec_ref):
    c = RET_CHUNK
    dr = pl.program_id(1)

    @pl.when(pl.program_id(2) == 0)
    def _init_direction():
        state_ref[...] = jnp.zeros_like(state_ref)
        idx = lax.broadcasted_iota(jnp.int32, (c, LANES), 0).astype(F32)
        expo = jnp.where(dr == 0, c - 1.0 - idx, idx)
        for hd in range(N_HEADS):
            kdec_ref[hd] = jnp.exp(_log_sigmoid(jnp.full((c, LANES), dec_ref[dr, hd], F32)) * expo)
            cdec_ref[hd] = jnp.exp(
                _log_sigmoid(jnp.full((F32_SUBLANES, HEAD_W), dec_ref[dr, hd], F32)) * c)

    for hd in range(N_HEADS):
        cols = _head_cols(hd)
        state = state_ref[hd]
        o_ref[0, 0, 0, cols, :] = state.astype(BF16)
        kdec = kdec_ref[hd]
        kf = k_ref[0, :, cols].astype(F32)
        kd = jnp.concatenate([kf[:, :LANES] * kdec, kf[:, LANES:] * kdec], axis=1).astype(BF16)
        kv = lax.dot_general(kd, v_ref[0, :, cols], (((0,), (0,)), ((), ())),
                             preferred_element_type=F32)
        state_ref[hd] = state * cdec_ref[hd, 0:1, :] + kv


def _ret_state(dec, k, v):
    batch, seq, _ = k.shape
    c = RET_CHUNK
    n_chunks = seq // c

    def chunk_idx(dr, t):
        return jnp.where(dr == 0, t, n_chunks - 1 - t)

    in_spec = pl.BlockSpec((1, c, D_MODEL), lambda b, dr, t: (b, chunk_idx(dr, t), 0))
    return pl.pallas_call(
        _ret_state_kernel,
        out_shape=jax.ShapeDtypeStruct((batch, n_chunks, 2, D_MODEL, HEAD_W), BF16),
        grid=(batch, 2, n_chunks),
        in_specs=[pl.BlockSpec(memory_space=pltpu.SMEM), in_spec, in_spec],
        out_specs=pl.BlockSpec((1, 1, 1, D_MODEL, HEAD_W),
                               lambda b, dr, t: (b, chunk_idx(dr, t), dr, 0, 0)),
        scratch_shapes=[
            pltpu.VMEM((N_HEADS, HEAD_W, HEAD_W), F32),
            pltpu.VMEM((N_HEADS, c, LANES), F32),
            pltpu.VMEM((N_HEADS, F32_SUBLANES, HEAD_W), F32),
        ],
        compiler_params=pltpu.CompilerParams(
            dimension_semantics=("arbitrary",) * 3, vmem_limit_bytes=VMEM_LIMIT_BYTES),
        name="ret_state",
    )(dec, k, v)


def _retention_kernel(dec_ref, q_ref, k_ref, v_ref, sg_ref, st_ref, gn_ref, o_ref,
                      dmask_ref, qdec_ref):
    c = RET_CHUNK

    @pl.when(jnp.logical_and(pl.program_id(0) == 0, pl.program_id(1) == 0))
    def _init_decay():
        diff = (lax.broadcasted_iota(jnp.int32, (c, c), 0)
                - lax.broadcasted_iota(jnp.int32, (c, c), 1)).astype(F32)
        idx = lax.broadcasted_iota(jnp.int32, (c, LANES), 0).astype(F32)
        for hd in range(N_HEADS):
            lg_f = _log_sigmoid(jnp.full((c, c), dec_ref[0, hd], F32))
            lg_b = _log_sigmoid(jnp.full((c, c), dec_ref[1, hd], F32))
            dmask_ref[hd] = jnp.where(diff >= 0, jnp.exp(lg_f * jnp.maximum(diff, 0.0)),
                                      jnp.exp(lg_b * jnp.maximum(-diff, 0.0)))
            qdec_ref[0, hd] = jnp.exp(_log_sigmoid(jnp.full((c, LANES), dec_ref[0, hd], F32)) * (idx + 1.0))
            qdec_ref[1, hd] = jnp.exp(_log_sigmoid(jnp.full((c, LANES), dec_ref[1, hd], F32)) * (c - idx))

    def cross(q, dr, hd):
        y = jnp.dot(q, st_ref[0, 0, dr, _head_cols(hd), :], preferred_element_type=F32)
        qd = qdec_ref[dr, hd]
        return jnp.concatenate([y[:, :LANES] * qd, y[:, LANES:] * qd], axis=1)

    for hd in range(N_HEADS):
        cols = _head_cols(hd)
        q = q_ref[0, :, cols]
        v = v_ref[0, :, cols]
        s = lax.dot_general(q, k_ref[0, :, cols], (((1,), (1,)), ((), ())),
                            preferred_element_type=F32)
        p = (s * dmask_ref[hd]).astype(BF16)
        y = jnp.dot(p, v, preferred_element_type=F32) + cross(q, 0, hd) + cross(q, 1, hd)
        mu = jnp.sum(y[:, :LANES] + y[:, LANES:], axis=-1, keepdims=True) * (1.0 / HEAD_W)
        yc = y - mu
        sq = yc * yc
        var = jnp.sum(sq[:, :LANES] + sq[:, LANES:], axis=-1, keepdims=True) * (1.0 / HEAD_W)
        yn = yc * lax.rsqrt(var + EPS) * gn_ref[:, cols]
        o_ref[0, :, cols] = (sg_ref[0, :, cols].astype(F32) * yn).astype(BF16)


def _retention(dec, q, k, v, sg, states, ret_gn):
    batch, seq, _ = q.shape
    c = RET_CHUNK
    row_spec = pl.BlockSpec((1, c, D_MODEL), lambda b, t: (b, t, 0))
    return pl.pallas_call(
        _retention_kernel,
        out_shape=jax.ShapeDtypeStruct(q.shape, BF16),
        grid=(batch, seq // c),
        in_specs=[
            pl.BlockSpec(memory_space=pltpu.SMEM),
            row_spec, row_spec, row_spec, row_spec,
            pl.BlockSpec((1, 1, 2, D_MODEL, HEAD_W), lambda b, t: (b, t, 0, 0, 0)),
            _const_spec(ret_gn.shape),
        ],
        out_specs=row_spec,
        scratch_shapes=[
            pltpu.VMEM((N_HEADS, c, c), F32),
            pltpu.VMEM((2, N_HEADS, c, LANES), F32),
        ],
        compiler_params=pltpu.CompilerParams(
            dimension_semantics=("arbitrary", "arbitrary"), vmem_limit_bytes=VMEM_LIMIT_BYTES),
        name="retention",
    )(dec, q, k, v, sg, states, ret_gn)


def _mix_out_kernel(x_ref, yr_ref, p_ref, pprev_ref, pnext_ref, mq_ref, gt_ref, kv_ref,
                    w_ret_ref, pool_w_ref, pool_scale_ref, w_pool_ref, w_mem_ref, w_o_ref, g_ref,
                    o_ref, pext_ref, *, seq):
    tm = ROW_TILE
    halo = BF16_SUBLANES
    t = pl.program_id(1)
    has_prev = (t > 0).astype(F32)
    has_next = (t < pl.num_programs(1) - 1).astype(F32)

    merged = gt_ref[0, :, 0:D_MODEL].astype(F32) * jnp.dot(
        yr_ref[0], w_ret_ref[...], preferred_element_type=F32)

    pprev = pprev_ref[0].astype(F32) * has_prev
    pmain = p_ref[0].astype(F32)
    pnext = pnext_ref[0].astype(F32) * has_next
    for ct in range(D_MODEL // LANES):
        lanes = slice(ct * LANES, (ct + 1) * LANES)
        pext_ref[ct, 0:halo, :] = pprev[:, lanes]
        pext_ref[ct, halo:halo + tm, :] = pmain[:, lanes]
        pext_ref[ct, halo + tm:, :] = pnext[:, lanes]
    pos = t * tm + lax.broadcasted_iota(jnp.int32, (tm, LANES), 0)
    y_pool = None
    for gi, w in enumerate(POOL_WINDOWS):
        cols = slice(gi * HEAD_W, (gi + 1) * HEAD_W)
        cnt = (jnp.minimum(pos + w // 2, seq) - jnp.maximum(pos - w // 2, 0)).astype(F32)
        d = []
        for ct in range(gi * HEAD_W // LANES, (gi + 1) * HEAD_W // LANES):
            acc = pext_ref[ct, halo - w // 2:halo - w // 2 + tm, :]
            for j in range(1, w):
                acc = acc + pext_ref[ct, halo - w // 2 + j:halo - w // 2 + j + tm, :]
            d.append(acc / cnt - pext_ref[ct, halo:halo + tm, :])
        d = jnp.concatenate(d, axis=1).astype(BF16)
        yg = jnp.dot(d, pool_w_ref[gi], preferred_element_type=F32) * pool_scale_ref[:, cols]
        part = jnp.dot(yg.astype(BF16), w_pool_ref[cols, :], preferred_element_type=F32)
        y_pool = part if y_pool is None else y_pool + part
    merged = merged + gt_ref[0, :, D_MODEL:2 * D_MODEL].astype(F32) * y_pool

    y_mem = None
    for hd in range(N_HEADS):
        cols = slice(hd * HEAD_W, (hd + 1) * HEAD_W)
        kh = kv_ref[0, :, hd * HEAD_W:(hd + 1) * HEAD_W]
        vh = kv_ref[0, :, D_MODEL + hd * HEAD_W:D_MODEL + (hd + 1) * HEAD_W]
        s = lax.dot_general(mq_ref[0, :, cols], kh, (((1,), (1,)), ((), ())),
                            preferred_element_type=F32) * QK_SCALE
        e = jnp.exp(s - jnp.max(s, axis=-1, keepdims=True))
        probs = (e / jnp.sum(e, axis=-1, keepdims=True)).astype(BF16)
        oh = jnp.dot(probs, vh, preferred_element_type=F32).astype(BF16)
        part = jnp.dot(oh, w_mem_ref[cols, :], preferred_element_type=F32)
        y_mem = part if y_mem is None else y_mem + part
    merged = merged + gt_ref[0, :, 2 * D_MODEL:].astype(F32) * y_mem

    y = jnp.dot(merged.astype(BF16), w_o_ref[...], preferred_element_type=F32)
    o_ref[0] = x_ref[0] + _rms(y, g_ref[...])


def _mix_out(x, yr, p, mq, gates, kv, w_ret, pool_w, pool_scale, w_pool, w_mem, w_o, g_post):
    batch, seq, _ = x.shape
    tm = ROW_TILE
    halo = BF16_SUBLANES
    n_tiles = seq // tm
    halo_per_tile = tm // halo
    n_halo_blocks = seq // halo
    row_spec = pl.BlockSpec((1, tm, D_MODEL), lambda b, t: (b, t, 0))
    prev_spec = pl.BlockSpec((1, halo, D_MODEL),
                             lambda b, t: (b, jnp.maximum(t * halo_per_tile - 1, 0), 0))
    next_spec = pl.BlockSpec((1, halo, D_MODEL),
                             lambda b, t: (b, jnp.minimum((t + 1) * halo_per_tile, n_halo_blocks - 1), 0))
    return pl.pallas_call(
        functools.partial(_mix_out_kernel, seq=seq),
        out_shape=jax.ShapeDtypeStruct(x.shape, F32),
        grid=(batch, n_tiles),
        in_specs=[
            row_spec, row_spec, row_spec, prev_spec, next_spec, row_spec,
            pl.BlockSpec((1, tm, N_BRANCH * D_MODEL), lambda b, t: (b, t, 0)),
            pl.BlockSpec((1, MEM_TOKENS, 2 * D_MODEL), lambda b, t: (b, 0, 0)),
            _const_spec(w_ret.shape), _const_spec(pool_w.shape), _const_spec(pool_scale.shape),
            _const_spec(w_pool.shape), _const_spec(w_mem.shape), _const_spec(w_o.shape),
            _const_spec(g_post.shape),
        ],
        out_specs=row_spec,
        scratch_shapes=[pltpu.VMEM((D_MODEL // LANES, tm + 2 * halo, LANES), F32)],
        compiler_params=pltpu.CompilerParams(
            dimension_semantics=("parallel", "parallel"), vmem_limit_bytes=VMEM_LIMIT_BYTES),
        name="mix_out",
    )(x, yr, p, p, p, mq, gates, kv, w_ret, pool_w, pool_scale, w_pool, w_mem, w_o, g_post)


def _gelu_tanh(x):
    return 0.5 * x * (1.0 + jnp.tanh(0.7978845608028654 * (x + 0.044715 * (x * x * x))))


def _ffn_kernel(x_ref, xprev_ref, xnext_ref, g_pre_ref, w_up_ref, conv_w_ref, conv_b_ref,
                w_down_ref, g_post_ref, o_ref, u_ref):
    tm = ROW_TILE
    halo = F32_SUBLANES
    t = pl.program_id(1)
    has_prev = (t > 0).astype(F32)
    has_next = (t < pl.num_programs(1) - 1).astype(F32)
    x = x_ref[0]
    xext = jnp.concatenate([xprev_ref[0] * has_prev, x, xnext_ref[0] * has_next], axis=0)
    hext = _rms(xext, g_pre_ref[...]).astype(BF16)

    n_chunks = D_FF // FF_CHUNK

    lane_tiles = FF_CHUNK // LANES

    def up(j):
        for half in range(2):
            col0 = half * D_FF + j * FF_CHUNK
            u = jnp.dot(hext, w_up_ref[:, col0:col0 + FF_CHUNK], preferred_element_type=F32)
            for ct in range(lane_tiles):
                u_ref[j % FF_BUFS, half, ct] = u[:, ct * LANES:(ct + 1) * LANES]

    def conv(j, half):
        outs = []
        for ct in range(lane_tiles):
            u = u_ref.at[j % FF_BUFS, half, ct]
            col0 = half * D_FF + j * FF_CHUNK + ct * LANES
            cols = slice(col0, col0 + LANES)
            outs.append(u[halo - 1:halo - 1 + tm, :] * conv_w_ref[0:1, cols]
                        + u[halo:halo + tm, :] * conv_w_ref[1:2, cols]
                        + u[halo + 1:halo + 1 + tm, :] * conv_w_ref[2:3, cols]
                        + conv_b_ref[:, cols])
        return jnp.concatenate(outs, axis=1)

    acc = None
    for j in range(min(FF_BUFS - 1, n_chunks)):
        up(j)
    for j in range(n_chunks):
        if j + FF_BUFS - 1 < n_chunks:
            up(j + FF_BUFS - 1)
        a = (_gelu_tanh(conv(j, 0)) * conv(j, 1)).astype(BF16)
        part = jnp.dot(a, w_down_ref[j * FF_CHUNK:(j + 1) * FF_CHUNK, :], preferred_element_type=F32)
        acc = part if acc is None else acc + part
    o_ref[0] = x + _rms(acc, g_post_ref[...])


def _ffn(x, g_pre, w_up, conv_w, conv_b, w_down, g_post):
    batch, seq, _ = x.shape
    tm = ROW_TILE
    halo = F32_SUBLANES
    n_tiles = seq // tm
    halo_per_tile = tm // halo
    n_halo_blocks = seq // halo
    row_spec = pl.BlockSpec((1, tm, D_MODEL), lambda b, t: (b, t, 0))
    prev_spec = pl.BlockSpec((1, halo, D_MODEL),
                             lambda b, t: (b, jnp.maximum(t * halo_per_tile - 1, 0), 0))
    next_spec = pl.BlockSpec((1, halo, D_MODEL),
                             lambda b, t: (b, jnp.minimum((t + 1) * halo_per_tile, n_halo_blocks - 1), 0))
    return pl.pallas_call(
        _ffn_kernel,
        out_shape=jax.ShapeDtypeStruct(x.shape, F32),
        grid=(batch, n_tiles),
        in_specs=[
            row_spec, prev_spec, next_spec,
            _const_spec(g_pre.shape), _const_spec(w_up.shape), _const_spec(conv_w.shape),
            _const_spec(conv_b.shape), _const_spec(w_down.shape), _const_spec(g_post.shape),
        ],
        out_specs=row_spec,
        scratch_shapes=[pltpu.VMEM((FF_BUFS, 2, FF_CHUNK // LANES, tm + 2 * halo, LANES), F32)],
        compiler_params=pltpu.CompilerParams(
            dimension_semantics=("parallel", "parallel"), vmem_limit_bytes=VMEM_LIMIT_BYTES),
        name="ffn",
    )(x, x, x, g_pre, w_up, conv_w, conv_b, w_down, g_post)


def _rope_tables(seq):
    inv = ROPE_BASE ** (-jnp.arange(HALF_W, dtype=F32) / HALF_W)
    ang = jnp.arange(seq, dtype=F32)[:, None] * inv[None, :]
    return jnp.cos(ang), jnp.sin(ang)


def _layer(x, mem, w):
    batch, seq, _ = x.shape
    cos, sin = _rope_tables(seq)
    kv = _mem_kv(mem.reshape(batch * MEM_TOKENS, D_MODEL), w["g_mem"], w["w_mem_kv"])
    kv = kv.reshape(batch, MEM_TOKENS, 2 * D_MODEL)
    q, k, v, sg, p, mq, gates = _in_proj(
        x.reshape(batch * seq, D_MODEL), w["g_mix_pre"], w["w_in"], cos, sin, seq)
    shp = (batch, seq, D_MODEL)
    q, k, v, sg = (a.reshape(shp) for a in (q, k, v, sg))
    states = _ret_state(w["decay"], k, v)
    yr = _retention(w["decay"], q, k, v, sg, states, w["ret_gn"])
    x1 = _mix_out(x, yr, p.reshape(shp), mq.reshape(shp), gates.reshape(batch, seq, N_BRANCH * D_MODEL),
                  kv, w["w_ret_out"], w["pool_w"], w["pool_scale"], w["w_pool_out"], w["w_mem_out"],
                  w["w_o"], w["g_mix_post"])
    return _ffn(x1, w["g_ffn_pre"], w["w_up"], w["conv_w"], w["conv_b"], w["w_down"], w["g_ffn_post"])


def kernel(x_prompt, x_sample, mem_prompt, mem_sample, g_mix_pre, g_mix_post, g_mem, w_in,
           decay_fwd, decay_bwd, ret_gn, w_ret_out, pool_w, pool_scale, w_pool_out,
           w_mem_kv, w_mem_out, w_o, g_ffn_pre, g_ffn_post, w_up, conv_w, conv_b, w_down):
    depth = w_in.shape[0]
    for l in range(depth):
        w = {
            "g_mix_pre": g_mix_pre[l][None], "g_mix_post": g_mix_post[l][None], "g_mem": g_mem[l][None],
            "w_in": w_in[l].astype(BF16),
            "decay": jnp.stack([decay_fwd[l], decay_bwd[l]]).astype(F32),
            "ret_gn": ret_gn[l][None],
            "w_ret_out": w_ret_out[l].astype(BF16), "pool_w": pool_w[l].astype(BF16),
            "pool_scale": pool_scale[l][None], "w_pool_out": w_pool_out[l].astype(BF16),
            "w_mem_kv": w_mem_kv[l].astype(BF16), "w_mem_out": w_mem_out[l].astype(BF16),
            "w_o": w_o[l].astype(BF16),
            "g_ffn_pre": g_ffn_pre[l][None], "g_ffn_post": g_ffn_post[l][None],
            "w_up": w_up[l].astype(BF16), "conv_w": conv_w[l], "conv_b": conv_b[l][None],
            "w_down": w_down[l].astype(BF16),
        }
        x_prompt = _layer(x_prompt, mem_prompt, w)
        x_sample = _layer(x_sample, mem_sample, w)
    return (x_prompt, x_sample)
```

```python
import functools

import jax
import jax.numpy as jnp
from jax import lax
from jax.experimental import pallas as pl
from jax.experimental.pallas import tpu as pltpu

F32 = jnp.float32
BF16 = jnp.bfloat16

D_MODEL = 1024
N_HEADS = 4
HEAD_W = 256
HALF_W = HEAD_W // 2
RET_CHUNK = 512
ROPE_BASE = 10000.0
POOL_WINDOWS = (2, 4, 8, 16)
MEM_TOKENS = 256
N_BRANCH = 3
D_FF = 2816
EPS = 1e-6
QK_SCALE = HEAD_W ** -0.5

VMEM_LIMIT_BYTES = 56 * 1024 * 1024
LANES = 128
BF16_SUBLANES = 16
F32_SUBLANES = 8

ROW_TILE = 512
FF_CHUNK = 256
FF_BUFS = 3


def _rms(x, g):
    ms = jnp.mean(x * x, axis=-1, keepdims=True)
    return x * lax.rsqrt(ms + EPS) * g


def _sigmoid(x):
    return 1.0 / (1.0 + jnp.exp(-x))


def _const_spec(shape):
    zeros = (0,) * len(shape)
    return pl.BlockSpec(shape, lambda *_: zeros, pipeline_mode=pl.Buffered(1))


def _head_cols(hd):
    return slice(hd * HEAD_W, (hd + 1) * HEAD_W)


def _mem_kv_kernel(m_ref, g_ref, w_kv_ref, w_out_ref, k_ref, vw_ref):
    h = _rms(m_ref[0], g_ref[...]).astype(BF16)
    kv = jnp.dot(h, w_kv_ref[...], preferred_element_type=F32)
    k_ref[0] = kv[:, :D_MODEL].astype(BF16)
    for hd in range(N_HEADS):
        cols = _head_cols(hd)
        vh = kv[:, D_MODEL + hd * HEAD_W:D_MODEL + (hd + 1) * HEAD_W].astype(BF16)
        vw_ref[0, cols, :] = jnp.dot(vh, w_out_ref[cols, :], preferred_element_type=F32).astype(BF16)


def _mem_kv(mem, g_mem, w_kv, w_mem_out):
    batch = mem.shape[0]
    return pl.pallas_call(
        _mem_kv_kernel,
        out_shape=(jax.ShapeDtypeStruct((batch, MEM_TOKENS, D_MODEL), BF16),
                   jax.ShapeDtypeStruct((batch, N_HEADS * MEM_TOKENS, D_MODEL), BF16)),
        grid=(batch,),
        in_specs=[
            pl.BlockSpec((1, MEM_TOKENS, D_MODEL), lambda i: (i, 0, 0)),
            _const_spec((1, D_MODEL)),
            _const_spec(w_kv.shape),
            _const_spec(w_mem_out.shape),
        ],
        out_specs=(pl.BlockSpec((1, MEM_TOKENS, D_MODEL), lambda i: (i, 0, 0)),
                   pl.BlockSpec((1, N_HEADS * MEM_TOKENS, D_MODEL), lambda i: (i, 0, 0))),
        compiler_params=pltpu.CompilerParams(
            dimension_semantics=("parallel",), vmem_limit_bytes=VMEM_LIMIT_BYTES),
        name="mem_kv",
    )(mem, g_mem, w_kv, w_mem_out)


def _pool_fold_kernel(pw_ref, scale_ref, w_out_ref, o_ref):
    a = pw_ref[0] * scale_ref[...]
    o_ref[...] = jnp.dot(a, w_out_ref[...], preferred_element_type=F32,
                         precision=lax.Precision.HIGHEST).astype(BF16)


def _pool_fold(pool_w, pool_scale, w_pool_out):
    return pl.pallas_call(
        _pool_fold_kernel,
        out_shape=jax.ShapeDtypeStruct((D_MODEL, D_MODEL), BF16),
        grid=(N_HEADS,),
        in_specs=[
            pl.BlockSpec((1, HEAD_W, HEAD_W), lambda g: (g, 0, 0)),
            pl.BlockSpec((1, HEAD_W), lambda g: (0, g)),
            pl.BlockSpec((HEAD_W, D_MODEL), lambda g: (g, 0)),
        ],
        out_specs=pl.BlockSpec((HEAD_W, D_MODEL), lambda g: (g, 0)),
        compiler_params=pltpu.CompilerParams(
            dimension_semantics=("parallel",), vmem_limit_bytes=VMEM_LIMIT_BYTES),
        name="pool_fold",
    )(pool_w, pool_scale, w_pool_out)


def _in_proj_kernel(dec_ref, x_ref, g_ref, w_ref, cos_ref, sin_ref,
                    q_ref, k_ref, v_ref, sg_ref, p_ref, mq_ref, gt_ref, kv_ref, kdec_ref):
    c = RET_CHUNK

    @pl.when(pl.program_id(0) == 0)
    def _init_key_decay():
        idx = lax.broadcasted_iota(jnp.int32, (F32_SUBLANES, c), 1).astype(F32)
        for hd in range(N_HEADS):
            kdec_ref[0, hd] = jnp.exp(
                _log_sigmoid(jnp.full((F32_SUBLANES, c), dec_ref[0, hd], F32)) * (c - 1.0 - idx))
            kdec_ref[1, hd] = jnp.exp(
                _log_sigmoid(jnp.full((F32_SUBLANES, c), dec_ref[1, hd], F32)) * idx)

    h = _rms(x_ref[...], g_ref[...]).astype(BF16)
    cos = cos_ref[...]
    sin = sin_ref[...]

    def seg(i):
        return jnp.dot(h, w_ref[:, i * D_MODEL:(i + 1) * D_MODEL], preferred_element_type=F32)

    def rotary(z, hd, scale):
        lo = hd * HEAD_W
        x1 = z[:, lo:lo + HALF_W]
        x2 = z[:, lo + HALF_W:lo + HEAD_W]
        return jnp.concatenate([(x1 * cos - x2 * sin) * scale, (x1 * sin + x2 * cos) * scale], axis=1)

    for b in range(N_BRANCH):
        gt_ref[:, b * D_MODEL:(b + 1) * D_MODEL] = _sigmoid(seg(6 + b)).astype(BF16)
    rg = seg(3)
    sg_ref[...] = (rg * _sigmoid(rg)).astype(BF16)
    v = seg(2).astype(BF16)
    v_ref[...] = v
    zk = seg(1)
    for hd in range(N_HEADS):
        cols = _head_cols(hd)
        kh = rotary(zk, hd, QK_SCALE)
        k_ref[:, cols] = kh.astype(BF16)
        kt = kh.T
        for dr in range(2):
            kd = (kt * kdec_ref[dr, hd, 0:1, :]).astype(BF16)
            kv_ref[0, dr, cols, :] = jnp.dot(kd, v[:, cols], preferred_element_type=F32)
    zq = seg(0)
    for hd in range(N_HEADS):
        q_ref[:, _head_cols(hd)] = rotary(zq, hd, 1.0).astype(BF16)
    p_ref[...] = seg(4).astype(BF16)
    mq_ref[...] = seg(5).astype(BF16)


def _in_proj(dec, x2d, g, w_in, cos, sin, seq):
    rows = x2d.shape[0]
    tm = ROW_TILE
    assert tm == RET_CHUNK
    tiles_per_seq = seq // tm
    n_tiles = rows // tm
    row_spec = pl.BlockSpec((tm, D_MODEL), lambda i: (i, 0))
    tab_spec = pl.BlockSpec((tm, HALF_W), lambda i: (i % tiles_per_seq, 0))
    out1 = jax.ShapeDtypeStruct((rows, D_MODEL), BF16)
    return pl.pallas_call(
        _in_proj_kernel,
        out_shape=(out1,) * 6 + (jax.ShapeDtypeStruct((rows, N_BRANCH * D_MODEL), BF16),
                                 jax.ShapeDtypeStruct((n_tiles, 2, D_MODEL, HEAD_W), F32)),
        grid=(n_tiles,),
        in_specs=[pl.BlockSpec(memory_space=pltpu.SMEM),
                  row_spec, _const_spec((1, D_MODEL)), _const_spec(w_in.shape), tab_spec, tab_spec],
        out_specs=(row_spec,) * 6 + (pl.BlockSpec((tm, N_BRANCH * D_MODEL), lambda i: (i, 0)),
                                     pl.BlockSpec((1, 2, D_MODEL, HEAD_W), lambda i: (i, 0, 0, 0))),
        scratch_shapes=[pltpu.VMEM((2, N_HEADS, F32_SUBLANES, RET_CHUNK), F32)],
        compiler_params=pltpu.CompilerParams(
            dimension_semantics=("arbitrary",), vmem_limit_bytes=VMEM_LIMIT_BYTES),
        name="in_proj",
    )(dec, x2d, g, w_in, cos, sin)


def _log_sigmoid(x):
    return -(jnp.maximum(-x, 0.0) + jnp.log1p(jnp.exp(-jnp.abs(x))))


def _ret_state_kernel(dec_ref, kv_ref, o_ref, state_ref, cdec_ref):
    dr = pl.program_id(1)

    @pl.when(pl.program_id(2) == 0)
    def _init_direction():
        state_ref[...] = jnp.zeros_like(state_ref)
        for hd in range(N_HEADS):
            cdec_ref[hd] = jnp.exp(
                _log_sigmoid(jnp.full((F32_SUBLANES, HEAD_W), dec_ref[dr, hd], F32)) * RET_CHUNK)

    for hd in range(N_HEADS):
        cols = _head_cols(hd)
        state = state_ref[cols, :]
        o_ref[0, 0, 0, cols, :] = state.astype(BF16)
        state_ref[cols, :] = state * cdec_ref[hd, 0:1, :] + kv_ref[0, 0, cols, :]


def _ret_state(dec, kv, batch):
    n_chunks = kv.shape[0] // batch

    def chunk_idx(dr, t):
        return jnp.where(dr == 0, t, n_chunks - 1 - t)

    return pl.pallas_call(
        _ret_state_kernel,
        out_shape=jax.ShapeDtypeStruct((batch, n_chunks, 2, D_MODEL, HEAD_W), BF16),
        grid=(batch, 2, n_chunks),
        in_specs=[pl.BlockSpec(memory_space=pltpu.SMEM),
                  pl.BlockSpec((1, 1, D_MODEL, HEAD_W),
                               lambda b, dr, t: (b * n_chunks + chunk_idx(dr, t), dr, 0, 0))],
        out_specs=pl.BlockSpec((1, 1, 1, D_MODEL, HEAD_W),
                               lambda b, dr, t: (b, chunk_idx(dr, t), dr, 0, 0)),
        scratch_shapes=[
            pltpu.VMEM((D_MODEL, HEAD_W), F32),
            pltpu.VMEM((N_HEADS, F32_SUBLANES, HEAD_W), F32),
        ],
        compiler_params=pltpu.CompilerParams(
            dimension_semantics=("arbitrary",) * 3, vmem_limit_bytes=VMEM_LIMIT_BYTES),
        name="ret_state",
    )(dec, kv)


def _retention_kernel(dec_ref, q_ref, k_ref, v_ref, sg_ref, st_ref, gn_ref, o_ref,
                      dmask_ref, qdec_ref):
    c = RET_CHUNK

    @pl.when(jnp.logical_and(pl.program_id(0) == 0, pl.program_id(1) == 0))
    def _init_decay():
        diff = (lax.broadcasted_iota(jnp.int32, (c, c), 0)
                - lax.broadcasted_iota(jnp.int32, (c, c), 1)).astype(F32)
        idx = lax.broadcasted_iota(jnp.int32, (c, LANES), 0).astype(F32)
        for hd in range(N_HEADS):
            lg_f = _log_sigmoid(jnp.full((c, c), dec_ref[0, hd], F32))
            lg_b = _log_sigmoid(jnp.full((c, c), dec_ref[1, hd], F32))
            dmask_ref[hd] = jnp.where(diff >= 0, jnp.exp(lg_f * jnp.maximum(diff, 0.0)),
                                      jnp.exp(lg_b * jnp.maximum(-diff, 0.0)))
            qdec_ref[0, hd] = jnp.exp(_log_sigmoid(jnp.full((c, LANES), dec_ref[0, hd], F32)) * (idx + 1.0))
            qdec_ref[1, hd] = jnp.exp(_log_sigmoid(jnp.full((c, LANES), dec_ref[1, hd], F32)) * (c - idx))

    def cross(q, dr, hd):
        y = jnp.dot(q, st_ref[0, 0, dr, _head_cols(hd), :], preferred_element_type=F32)
        qd = qdec_ref[dr, hd]
        return jnp.concatenate([y[:, :LANES] * qd, y[:, LANES:] * qd], axis=1)

    for hd in range(N_HEADS):
        cols = _head_cols(hd)
        q = q_ref[0, :, cols]
        v = v_ref[0, :, cols]
        s = lax.dot_general(q, k_ref[0, :, cols], (((1,), (1,)), ((), ())),
                            preferred_element_type=F32)
        p = (s * dmask_ref[hd]).astype(BF16)
        y = jnp.dot(p, v, preferred_element_type=F32) + cross(q, 0, hd) + cross(q, 1, hd)
        mu = jnp.sum(y[:, :LANES] + y[:, LANES:], axis=-1, keepdims=True) * (1.0 / HEAD_W)
        yc = y - mu
        sq = yc * yc
        var = jnp.sum(sq[:, :LANES] + sq[:, LANES:], axis=-1, keepdims=True) * (1.0 / HEAD_W)
        yn = yc * lax.rsqrt(var + EPS) * gn_ref[:, cols]
        o_ref[0, :, cols] = (sg_ref[0, :, cols].astype(F32) * yn).astype(BF16)


def _retention(dec, q, k, v, sg, states, ret_gn):
    batch, seq, _ = q.shape
    c = RET_CHUNK
    row_spec = pl.BlockSpec((1, c, D_MODEL), lambda b, t: (b, t, 0))
    return pl.pallas_call(
        _retention_kernel,
        out_shape=jax.ShapeDtypeStruct(q.shape, BF16),
        grid=(batch, seq // c),
        in_specs=[
            pl.BlockSpec(memory_space=pltpu.SMEM),
            row_spec, row_spec, row_spec, row_spec,
            pl.BlockSpec((1, 1, 2, D_MODEL, HEAD_W), lambda b, t: (b, t, 0, 0, 0)),
            _const_spec(ret_gn.shape),
        ],
        out_specs=row_spec,
        scratch_shapes=[
            pltpu.VMEM((N_HEADS, c, c), F32),
            pltpu.VMEM((2, N_HEADS, c, LANES), F32),
        ],
        compiler_params=pltpu.CompilerParams(
            dimension_semantics=("arbitrary", "arbitrary"), vmem_limit_bytes=VMEM_LIMIT_BYTES),
        name="retention",
    )(dec, q, k, v, sg, states, ret_gn)


def _mix_out_kernel(x_ref, yr_ref, p_ref, pprev_ref, pnext_ref, mq_ref, gt_ref, km_ref, vw_ref,
                    w_ret_ref, w_pool_ref, w_o_ref, g_ref, o_ref, pext_ref, *, seq):
    tm = ROW_TILE
    halo = BF16_SUBLANES
    t = pl.program_id(1)
    has_prev = (t > 0).astype(F32)
    has_next = (t < pl.num_programs(1) - 1).astype(F32)

    merged = gt_ref[0, :, 0:D_MODEL].astype(F32) * jnp.dot(
        yr_ref[0], w_ret_ref[...], preferred_element_type=F32)

    pprev = pprev_ref[0].astype(F32) * has_prev
    pmain = p_ref[0].astype(F32)
    pnext = pnext_ref[0].astype(F32) * has_next
    for ct in range(D_MODEL // LANES):
        lanes = slice(ct * LANES, (ct + 1) * LANES)
        pext_ref[ct, 0:halo, :] = pprev[:, lanes]
        pext_ref[ct, halo:halo + tm, :] = pmain[:, lanes]
        pext_ref[ct, halo + tm:, :] = pnext[:, lanes]
    pos = t * tm + lax.broadcasted_iota(jnp.int32, (tm, LANES), 0)
    d = []
    for gi, w in enumerate(POOL_WINDOWS):
        cnt = (jnp.minimum(pos + w // 2, seq) - jnp.maximum(pos - w // 2, 0)).astype(F32)
        inv_cnt = 1.0 / cnt
        for ct in range(gi * HEAD_W // LANES, (gi + 1) * HEAD_W // LANES):
            acc = pext_ref[ct, halo - w // 2:halo - w // 2 + tm, :]
            for j in range(1, w):
                acc = acc + pext_ref[ct, halo - w // 2 + j:halo - w // 2 + j + tm, :]
            d.append(acc * inv_cnt - pext_ref[ct, halo:halo + tm, :])
    d = jnp.concatenate(d, axis=1).astype(BF16)
    y_pool = jnp.dot(d, w_pool_ref[...], preferred_element_type=F32)
    merged = merged + gt_ref[0, :, D_MODEL:2 * D_MODEL].astype(F32) * y_pool

    probs = []
    for hd in range(N_HEADS):
        cols = _head_cols(hd)
        s = lax.dot_general(mq_ref[0, :, cols], km_ref[0, :, cols], (((1,), (1,)), ((), ())),
                            preferred_element_type=F32)
        e = jnp.exp((s - jnp.max(s, axis=-1, keepdims=True)) * QK_SCALE)
        probs.append((e * (1.0 / jnp.sum(e, axis=-1, keepdims=True))).astype(BF16))
    y_mem = jnp.dot(jnp.concatenate(probs, axis=1), vw_ref[0], preferred_element_type=F32)
    merged = merged + gt_ref[0, :, 2 * D_MODEL:].astype(F32) * y_mem

    y = jnp.dot(merged.astype(BF16), w_o_ref[...], preferred_element_type=F32)
    o_ref[0] = x_ref[0] + _rms(y, g_ref[...])


def _mix_out(x, yr, p, mq, gates, k_mem, vw_mem, w_ret, w_pool, w_o, g_post):
    batch, seq, _ = x.shape
    tm = ROW_TILE
    halo = BF16_SUBLANES
    n_tiles = seq // tm
    halo_per_tile = tm // halo
    n_halo_blocks = seq // halo
    row_spec = pl.BlockSpec((1, tm, D_MODEL), lambda b, t: (b, t, 0))
    prev_spec = pl.BlockSpec((1, halo, D_MODEL),
                             lambda b, t: (b, jnp.maximum(t * halo_per_tile - 1, 0), 0))
    next_spec = pl.BlockSpec((1, halo, D_MODEL),
                             lambda b, t: (b, jnp.minimum((t + 1) * halo_per_tile, n_halo_blocks - 1), 0))
    return pl.pallas_call(
        functools.partial(_mix_out_kernel, seq=seq),
        out_shape=jax.ShapeDtypeStruct(x.shape, F32),
        grid=(batch, n_tiles),
        in_specs=[
            row_spec, row_spec, row_spec, prev_spec, next_spec, row_spec,
            pl.BlockSpec((1, tm, N_BRANCH * D_MODEL), lambda b, t: (b, t, 0)),
            pl.BlockSpec((1, MEM_TOKENS, D_MODEL), lambda b, t: (b, 0, 0)),
            pl.BlockSpec((1, N_HEADS * MEM_TOKENS, D_MODEL), lambda b, t: (b, 0, 0)),
            _const_spec(w_ret.shape), _const_spec(w_pool.shape), _const_spec(w_o.shape),
            _const_spec(g_post.shape),
        ],
        out_specs=row_spec,
        scratch_shapes=[pltpu.VMEM((D_MODEL // LANES, tm + 2 * halo, LANES), F32)],
        compiler_params=pltpu.CompilerParams(
            dimension_semantics=("parallel", "parallel"), vmem_limit_bytes=VMEM_LIMIT_BYTES),
        name="mix_out",
    )(x, yr, p, p, p, mq, gates, k_mem, vw_mem, w_ret, w_pool, w_o, g_post)


def _gelu_tanh(x):
    return 0.5 * x * (1.0 + jnp.tanh(0.7978845608028654 * (x + 0.044715 * (x * x * x))))


def _ffn_kernel(x_ref, xprev_ref, xnext_ref, g_pre_ref, w_up_ref, conv_w_ref, conv_b_ref,
                w_down_ref, g_post_ref, o_ref, u_ref):
    tm = ROW_TILE
    halo = F32_SUBLANES
    t = pl.program_id(1)
    has_prev = (t > 0).astype(F32)
    has_next = (t < pl.num_programs(1) - 1).astype(F32)
    x = x_ref[0]
    xext = jnp.concatenate([xprev_ref[0] * has_prev, x, xnext_ref[0] * has_next], axis=0)
    hext = _rms(xext, g_pre_ref[...]).astype(BF16)

    n_chunks = D_FF // FF_CHUNK

    lane_tiles = FF_CHUNK // LANES

    def up(j):
        for half in range(2):
            col0 = half * D_FF + j * FF_CHUNK
            u = jnp.dot(hext, w_up_ref[:, col0:col0 + FF_CHUNK], preferred_element_type=F32)
            for ct in range(lane_tiles):
                u_ref[j % FF_BUFS, half, ct] = u[:, ct * LANES:(ct + 1) * LANES]

    def conv(j, half):
        outs = []
        for ct in range(lane_tiles):
            u = u_ref.at[j % FF_BUFS, half, ct]
            col0 = half * D_FF + j * FF_CHUNK + ct * LANES
            cols = slice(col0, col0 + LANES)
            outs.append(u[halo - 1:halo - 1 + tm, :] * conv_w_ref[0:1, cols]
                        + u[halo:halo + tm, :] * conv_w_ref[1:2, cols]
                        + u[halo + 1:halo + 1 + tm, :] * conv_w_ref[2:3, cols]
                        + conv_b_ref[:, cols])
        return jnp.concatenate(outs, axis=1)

    acc = None
    for j in range(min(FF_BUFS - 1, n_chunks)):
        up(j)
    for j in range(n_chunks):
        if j + FF_BUFS - 1 < n_chunks:
            up(j + FF_BUFS - 1)
        a = (_gelu_tanh(conv(j, 0)) * conv(j, 1)).astype(BF16)
        part = jnp.dot(a, w_down_ref[j * FF_CHUNK:(j + 1) * FF_CHUNK, :], preferred_element_type=F32)
        acc = part if acc is None else acc + part
    o_ref[0] = x + _rms(acc, g_post_ref[...])


def _ffn(x, g_pre, w_up, conv_w, conv_b, w_down, g_post):
    batch, seq, _ = x.shape
    tm = ROW_TILE
    halo = F32_SUBLANES
    n_tiles = seq // tm
    halo_per_tile = tm // halo
    n_halo_blocks = seq // halo
    row_spec = pl.BlockSpec((1, tm, D_MODEL), lambda b, t: (b, t, 0))
    prev_spec = pl.BlockSpec((1, halo, D_MODEL),
                             lambda b, t: (b, jnp.maximum(t * halo_per_tile - 1, 0), 0))
    next_spec = pl.BlockSpec((1, halo, D_MODEL),
                             lambda b, t: (b, jnp.minimum((t + 1) * halo_per_tile, n_halo_blocks - 1), 0))
    return pl.pallas_call(
        _ffn_kernel,
        out_shape=jax.ShapeDtypeStruct(x.shape, F32),
        grid=(batch, n_tiles),
        in_specs=[
            row_spec, prev_spec, next_spec,
            _const_spec(g_pre.shape), _const_spec(w_up.shape), _const_spec(conv_w.shape),
            _const_spec(conv_b.shape), _const_spec(w_down.shape), _const_spec(g_post.shape),
        ],
        out_specs=row_spec,
        scratch_shapes=[pltpu.VMEM((FF_BUFS, 2, FF_CHUNK // LANES, tm + 2 * halo, LANES), F32)],
        compiler_params=pltpu.CompilerParams(
            dimension_semantics=("parallel", "parallel"), vmem_limit_bytes=VMEM_LIMIT_BYTES),
        name="ffn",
    )(x, x, x, g_pre, w_up, conv_w, conv_b, w_down, g_post)


def _rope_tables(seq):
    inv = ROPE_BASE ** (-jnp.arange(HALF_W, dtype=F32) / HALF_W)
    ang = jnp.arange(seq, dtype=F32)[:, None] * inv[None, :]
    return jnp.cos(ang), jnp.sin(ang)


def _layer(x, mem, w):
    batch, seq, _ = x.shape
    cos, sin = _rope_tables(seq)
    k_mem, vw_mem = _mem_kv(mem, w["g_mem"], w["w_mem_kv"], w["w_mem_out"])
    q, k, v, sg, p, mq, gates, kv = _in_proj(
        w["decay"], x.reshape(batch * seq, D_MODEL), w["g_mix_pre"], w["w_in"], cos, sin, seq)
    shp = (batch, seq, D_MODEL)
    q, k, v, sg = (a.reshape(shp) for a in (q, k, v, sg))
    states = _ret_state(w["decay"], kv, batch)
    yr = _retention(w["decay"], q, k, v, sg, states, w["ret_gn"])
    x1 = _mix_out(x, yr, p.reshape(shp), mq.reshape(shp), gates.reshape(batch, seq, N_BRANCH * D_MODEL),
                  k_mem, vw_mem, w["w_ret_out"], w["w_pool_fold"], w["w_o"], w["g_mix_post"])
    return _ffn(x1, w["g_ffn_pre"], w["w_up"], w["conv_w"], w["conv_b"], w["w_down"], w["g_ffn_post"])


def kernel(x_prompt, x_sample, mem_prompt, mem_sample, g_mix_pre, g_mix_post, g_mem, w_in,
           decay_fwd, decay_bwd, ret_gn, w_ret_out, pool_w, pool_scale, w_pool_out,
           w_mem_kv, w_mem_out, w_o, g_ffn_pre, g_ffn_post, w_up, conv_w, conv_b, w_down):
    depth = w_in.shape[0]
    for l in range(depth):
        w = {
            "g_mix_pre": g_mix_pre[l][None], "g_mix_post": g_mix_post[l][None], "g_mem": g_mem[l][None],
            "w_in": w_in[l].astype(BF16),
            "decay": jnp.stack([decay_fwd[l], decay_bwd[l]]).astype(F32),
            "ret_gn": ret_gn[l][None],
            "w_ret_out": w_ret_out[l].astype(BF16),
            "w_pool_fold": _pool_fold(pool_w[l], pool_scale[l][None], w_pool_out[l]),
            "w_mem_kv": w_mem_kv[l].astype(BF16), "w_mem_out": w_mem_out[l].astype(BF16),
            "w_o": w_o[l].astype(BF16),
            "g_ffn_pre": g_ffn_pre[l][None], "g_ffn_post": g_ffn_post[l][None],
            "w_up": w_up[l].astype(BF16), "conv_w": conv_w[l], "conv_b": conv_b[l][None],
            "w_down": w_down[l].astype(BF16),
        }
        x_prompt = _layer(x_prompt, mem_prompt, w)
        x_sample = _layer(x_sample, mem_sample, w)
    return (x_prompt, x_sample)
```

```python
import functools

import jax
import jax.numpy as jnp
from jax import lax
from jax.experimental import pallas as pl
from jax.experimental.pallas import tpu as pltpu

F32 = jnp.float32
BF16 = jnp.bfloat16

D_MODEL = 1024
N_HEADS = 4
HEAD_W = 256
HALF_W = HEAD_W // 2
RET_CHUNK = 512
ROPE_BASE = 10000.0
POOL_WINDOWS = (2, 4, 8, 16)
MEM_TOKENS = 256
N_BRANCH = 3
D_FF = 2816
EPS = 1e-6
QK_SCALE = HEAD_W ** -0.5

VMEM_LIMIT_BYTES = 58 * 1024 * 1024
LANES = 128
BF16_SUBLANES = 16
F32_SUBLANES = 8

ROW_TILE = 512
FF_CHUNK = 256
FF_BUFS = 4
STATE_GROUP = 4


def _rms(x, g):
    ms = jnp.mean(x * x, axis=-1, keepdims=True)
    return x * lax.rsqrt(ms + EPS) * g


def _sigmoid(x):
    return 1.0 / (1.0 + jnp.exp(-x))


def _log_sigmoid(x):
    return -(jnp.maximum(-x, 0.0) + jnp.log1p(jnp.exp(-jnp.abs(x))))


def _head_cols(hd):
    return slice(hd * HEAD_W, (hd + 1) * HEAD_W)


def _const_spec(shape):
    zeros = (0,) * len(shape)
    return pl.BlockSpec(shape, lambda *_: zeros, pipeline_mode=pl.Buffered(1))


def _mem_kv_kernel(m_ref, g_ref, w_kv_ref, w_out_ref, k_ref, vw_ref):
    h = _rms(m_ref[0], g_ref[...]).astype(BF16)
    kv = jnp.dot(h, w_kv_ref[...], preferred_element_type=F32)
    k_ref[0] = kv[:, :D_MODEL].astype(BF16)
    for hd in range(N_HEADS):
        cols = _head_cols(hd)
        vh = kv[:, D_MODEL + hd * HEAD_W:D_MODEL + (hd + 1) * HEAD_W].astype(BF16)
        vw_ref[0, cols, :] = jnp.dot(vh, w_out_ref[cols, :], preferred_element_type=F32).astype(BF16)


def _mem_kv(mem, g_mem, w_kv, w_mem_out):
    batch = mem.shape[0]
    return pl.pallas_call(
        _mem_kv_kernel,
        out_shape=(jax.ShapeDtypeStruct((batch, MEM_TOKENS, D_MODEL), BF16),
                   jax.ShapeDtypeStruct((batch, N_HEADS * MEM_TOKENS, D_MODEL), BF16)),
        grid=(batch,),
        in_specs=[
            pl.BlockSpec((1, MEM_TOKENS, D_MODEL), lambda i: (i, 0, 0)),
            _const_spec((1, D_MODEL)),
            _const_spec(w_kv.shape),
            _const_spec(w_mem_out.shape),
        ],
        out_specs=(pl.BlockSpec((1, MEM_TOKENS, D_MODEL), lambda i: (i, 0, 0)),
                   pl.BlockSpec((1, N_HEADS * MEM_TOKENS, D_MODEL), lambda i: (i, 0, 0))),
        compiler_params=pltpu.CompilerParams(
            dimension_semantics=("parallel",), vmem_limit_bytes=VMEM_LIMIT_BYTES),
        name="mem_kv",
    )(mem, g_mem, w_kv, w_mem_out)


def _pool_fold_kernel(pw_ref, scale_ref, w_out_ref, o_ref):
    a = pw_ref[0] * scale_ref[...]
    o_ref[...] = jnp.dot(a, w_out_ref[...], preferred_element_type=F32,
                         precision=lax.Precision.HIGHEST).astype(BF16)


def _pool_fold(pool_w, pool_scale, w_pool_out):
    return pl.pallas_call(
        _pool_fold_kernel,
        out_shape=jax.ShapeDtypeStruct((D_MODEL, D_MODEL), BF16),
        grid=(N_HEADS,),
        in_specs=[
            pl.BlockSpec((1, HEAD_W, HEAD_W), lambda g: (g, 0, 0)),
            pl.BlockSpec((1, HEAD_W), lambda g: (0, g)),
            pl.BlockSpec((HEAD_W, D_MODEL), lambda g: (g, 0)),
        ],
        out_specs=pl.BlockSpec((HEAD_W, D_MODEL), lambda g: (g, 0)),
        compiler_params=pltpu.CompilerParams(
            dimension_semantics=("parallel",), vmem_limit_bytes=VMEM_LIMIT_BYTES),
        name="pool_fold",
    )(pool_w, pool_scale, w_pool_out)


def _in_proj_kernel(dec_ref, x_ref, g_ref, w_ref, cos_ref, sin_ref,
                    q_ref, k_ref, v_ref, sg_ref, p_ref, mq_ref, gt_ref, kv_ref, kdec_ref):
    c = RET_CHUNK

    @pl.when(pl.program_id(0) == 0)
    def _init_key_decay():
        idx = lax.broadcasted_iota(jnp.int32, (F32_SUBLANES, c), 1).astype(F32)
        for hd in range(N_HEADS):
            kdec_ref[0, hd] = jnp.exp(
                _log_sigmoid(jnp.full((F32_SUBLANES, c), dec_ref[0, hd], F32)) * (c - 1.0 - idx))
            kdec_ref[1, hd] = jnp.exp(
                _log_sigmoid(jnp.full((F32_SUBLANES, c), dec_ref[1, hd], F32)) * idx)

    h = _rms(x_ref[...], g_ref[...]).astype(BF16)
    cos = cos_ref[...]
    sin = sin_ref[...]

    def seg(i):
        return jnp.dot(h, w_ref[:, i * D_MODEL:(i + 1) * D_MODEL], preferred_element_type=F32)

    def rotary(z, hd, scale):
        lo = hd * HEAD_W
        x1 = z[:, lo:lo + HALF_W]
        x2 = z[:, lo + HALF_W:lo + HEAD_W]
        return jnp.concatenate([(x1 * cos - x2 * sin) * scale, (x1 * sin + x2 * cos) * scale], axis=1)

    for b in range(N_BRANCH):
        gt_ref[:, b * D_MODEL:(b + 1) * D_MODEL] = _sigmoid(seg(6 + b)).astype(BF16)
    rg = seg(3)
    sg_ref[...] = (rg * _sigmoid(rg)).astype(BF16)
    v = seg(2).astype(BF16)
    v_ref[...] = v
    zk = seg(1)
    for hd in range(N_HEADS):
        cols = _head_cols(hd)
        kh = rotary(zk, hd, QK_SCALE)
        k_ref[:, cols] = kh.astype(BF16)
        kt = kh.T
        for dr in range(2):
            kd = (kt * kdec_ref[dr, hd, 0:1, :]).astype(BF16)
            kv_ref[0, dr, cols, :] = jnp.dot(kd, v[:, cols], preferred_element_type=F32)
    zq = seg(0)
    for hd in range(N_HEADS):
        q_ref[:, _head_cols(hd)] = rotary(zq, hd, 1.0).astype(BF16)
    p_ref[...] = seg(4).astype(BF16)
    mq_ref[...] = seg(5).astype(BF16)


def _in_proj(dec, x2d, g, w_in, cos, sin, seq):
    rows = x2d.shape[0]
    tm = ROW_TILE
    assert tm == RET_CHUNK
    tiles_per_seq = seq // tm
    n_tiles = rows // tm
    row_spec = pl.BlockSpec((tm, D_MODEL), lambda i: (i, 0))
    tab_spec = pl.BlockSpec((tm, HALF_W), lambda i: (i % tiles_per_seq, 0))
    out1 = jax.ShapeDtypeStruct((rows, D_MODEL), BF16)
    return pl.pallas_call(
        _in_proj_kernel,
        out_shape=(out1,) * 6 + (jax.ShapeDtypeStruct((rows, N_BRANCH * D_MODEL), BF16),
                                 jax.ShapeDtypeStruct((n_tiles, 2, D_MODEL, HEAD_W), F32)),
        grid=(n_tiles,),
        in_specs=[pl.BlockSpec(memory_space=pltpu.SMEM),
                  row_spec, _const_spec((1, D_MODEL)), _const_spec(w_in.shape), tab_spec, tab_spec],
        out_specs=(row_spec,) * 6 + (pl.BlockSpec((tm, N_BRANCH * D_MODEL), lambda i: (i, 0)),
                                     pl.BlockSpec((1, 2, D_MODEL, HEAD_W), lambda i: (i, 0, 0, 0))),
        scratch_shapes=[pltpu.VMEM((2, N_HEADS, F32_SUBLANES, RET_CHUNK), F32)],
        compiler_params=pltpu.CompilerParams(
            dimension_semantics=("arbitrary",), vmem_limit_bytes=VMEM_LIMIT_BYTES),
        name="in_proj",
    )(dec, x2d, g, w_in, cos, sin)


def _ret_state_kernel(dec_ref, kvf_ref, kvb_ref, of_ref, ob_ref, sf_ref, sb_ref, cdec_ref):
    group = kvf_ref.shape[0]

    @pl.when(pl.program_id(1) == 0)
    def _init_sequence():
        sf_ref[...] = jnp.zeros_like(sf_ref)
        sb_ref[...] = jnp.zeros_like(sb_ref)
        for dr in range(2):
            for hd in range(N_HEADS):
                cdec_ref[dr, hd] = jnp.exp(
                    _log_sigmoid(jnp.full((F32_SUBLANES, HEAD_W), dec_ref[dr, hd], F32)) * RET_CHUNK)

    def step(dr, state_ref, kv_ref, o_ref, j):
        for hd in range(N_HEADS):
            cols = _head_cols(hd)
            state = state_ref[cols, :]
            o_ref[j, cols, :] = state.astype(BF16)
            state_ref[cols, :] = state * cdec_ref[dr, hd, 0:1, :] + kv_ref[j, 0, cols, :]

    for j in range(group):
        step(0, sf_ref, kvf_ref, of_ref, j)
    for j in reversed(range(group)):
        step(1, sb_ref, kvb_ref, ob_ref, j)


def _ret_state(dec, kv, batch):
    n_chunks = kv.shape[0] // batch
    group = min(STATE_GROUP, n_chunks)
    n_groups = n_chunks // group
    out = jax.ShapeDtypeStruct((batch * n_chunks, D_MODEL, HEAD_W), BF16)
    return pl.pallas_call(
        _ret_state_kernel,
        out_shape=(out, out),
        grid=(batch, n_groups),
        in_specs=[
            pl.BlockSpec(memory_space=pltpu.SMEM),
            pl.BlockSpec((group, 1, D_MODEL, HEAD_W), lambda b, t: (b * n_groups + t, 0, 0, 0)),
            pl.BlockSpec((group, 1, D_MODEL, HEAD_W),
                         lambda b, t: (b * n_groups + n_groups - 1 - t, 1, 0, 0)),
        ],
        out_specs=(
            pl.BlockSpec((group, D_MODEL, HEAD_W), lambda b, t: (b * n_groups + t, 0, 0)),
            pl.BlockSpec((group, D_MODEL, HEAD_W), lambda b, t: (b * n_groups + n_groups - 1 - t, 0, 0)),
        ),
        scratch_shapes=[
            pltpu.VMEM((D_MODEL, HEAD_W), F32),
            pltpu.VMEM((D_MODEL, HEAD_W), F32),
            pltpu.VMEM((2, N_HEADS, F32_SUBLANES, HEAD_W), F32),
        ],
        compiler_params=pltpu.CompilerParams(
            dimension_semantics=("arbitrary", "arbitrary"), vmem_limit_bytes=VMEM_LIMIT_BYTES),
        name="ret_state",
    )(dec, kv, kv)


def _mix_out_kernel(dec_ref, x_ref, q_ref, k_ref, v_ref, sg_ref, stf_ref, stb_ref, gn_ref,
                    p_ref, pprev_ref, pnext_ref, mq_ref, gt_ref, km_ref, vw_ref,
                    w_ret_ref, w_pool_ref, w_o_ref, g_ref, o_ref,
                    pext_ref, psum_ref, dmask_ref, qdec_ref, *, seq):
    tm = ROW_TILE
    c = RET_CHUNK
    halo = BF16_SUBLANES
    t = pl.program_id(1)
    has_prev = (t > 0).astype(F32)
    has_next = (t < pl.num_programs(1) - 1).astype(F32)

    @pl.when(jnp.logical_and(pl.program_id(0) == 0, t == 0))
    def _init_decay():
        diff = (lax.broadcasted_iota(jnp.int32, (c, c), 0)
                - lax.broadcasted_iota(jnp.int32, (c, c), 1)).astype(F32)
        idx = lax.broadcasted_iota(jnp.int32, (c, LANES), 0).astype(F32)
        for hd in range(N_HEADS):
            lg_f = _log_sigmoid(jnp.full((c, c), dec_ref[0, hd], F32))
            lg_b = _log_sigmoid(jnp.full((c, c), dec_ref[1, hd], F32))
            dmask_ref[hd] = jnp.where(diff >= 0, jnp.exp(lg_f * jnp.maximum(diff, 0.0)),
                                      jnp.exp(lg_b * jnp.maximum(-diff, 0.0)))
            qdec_ref[0, hd] = jnp.exp(_log_sigmoid(jnp.full((c, LANES), dec_ref[0, hd], F32)) * (idx + 1.0))
            qdec_ref[1, hd] = jnp.exp(_log_sigmoid(jnp.full((c, LANES), dec_ref[1, hd], F32)) * (c - idx))

    def cross(q, st_ref, dr, hd):
        y = jnp.dot(q, st_ref[0, _head_cols(hd), :], preferred_element_type=F32)
        qd = qdec_ref[dr, hd]
        return jnp.concatenate([y[:, :LANES] * qd, y[:, LANES:] * qd], axis=1)

    yr = []
    for hd in range(N_HEADS):
        cols = _head_cols(hd)
        q = q_ref[0, :, cols]
        s = lax.dot_general(q, k_ref[0, :, cols], (((1,), (1,)), ((), ())),
                            preferred_element_type=F32)
        pm = (s * dmask_ref[hd]).astype(BF16)
        y = (jnp.dot(pm, v_ref[0, :, cols], preferred_element_type=F32)
             + cross(q, stf_ref, 0, hd) + cross(q, stb_ref, 1, hd))
        mu = jnp.sum(y[:, :LANES] + y[:, LANES:], axis=-1, keepdims=True) * (1.0 / HEAD_W)
        yc = y - mu
        sq = yc * yc
        var = jnp.sum(sq[:, :LANES] + sq[:, LANES:], axis=-1, keepdims=True) * (1.0 / HEAD_W)
        yn = yc * lax.rsqrt(var + EPS) * gn_ref[:, cols]
        yr.append((sg_ref[0, :, cols].astype(F32) * yn).astype(BF16))
    merged = gt_ref[0, :, 0:D_MODEL].astype(F32) * jnp.dot(
        jnp.concatenate(yr, axis=1), w_ret_ref[...], preferred_element_type=F32)

    pprev = pprev_ref[0].astype(F32) * has_prev
    pmain = p_ref[0].astype(F32)
    pnext = pnext_ref[0].astype(F32) * has_next
    for ct in range(D_MODEL // LANES):
        lanes = slice(ct * LANES, (ct + 1) * LANES)
        pext_ref[ct, 0:halo, :] = pprev[:, lanes]
        pext_ref[ct, halo:halo + tm, :] = pmain[:, lanes]
        pext_ref[ct, halo + tm:, :] = pnext[:, lanes]
    pos = t * tm + lax.broadcasted_iota(jnp.int32, (tm, LANES), 0)
    d = []
    for gi, w in enumerate(POOL_WINDOWS):
        cnt = (jnp.minimum(pos + w // 2, seq) - jnp.maximum(pos - w // 2, 0)).astype(F32)
        inv_cnt = 1.0 / cnt
        for ct in range(gi * HEAD_W // LANES, (gi + 1) * HEAD_W // LANES):
            start = halo - w // 2
            rows = tm + w - 2
            acc = pext_ref[ct, start:start + rows, :] + pext_ref[ct, start + 1:start + 1 + rows, :]
            span = 2
            while span < w:
                psum_ref[ct % 2, 0:rows, :] = acc
                rows -= span
                acc = psum_ref[ct % 2, 0:rows, :] + psum_ref[ct % 2, span:span + rows, :]
                span *= 2
            d.append(acc * inv_cnt - pext_ref[ct, halo:halo + tm, :])
    d = jnp.concatenate(d, axis=1).astype(BF16)
    y_pool = jnp.dot(d, w_pool_ref[...], preferred_element_type=F32)
    merged = merged + gt_ref[0, :, D_MODEL:2 * D_MODEL].astype(F32) * y_pool

    probs = []
    for hd in range(N_HEADS):
        cols = _head_cols(hd)
        s = lax.dot_general(mq_ref[0, :, cols], km_ref[0, :, cols], (((1,), (1,)), ((), ())),
                            preferred_element_type=F32)
        e = jnp.exp((s - jnp.max(s, axis=-1, keepdims=True)) * QK_SCALE)
        probs.append((e * (1.0 / jnp.sum(e, axis=-1, keepdims=True))).astype(BF16))
    y_mem = jnp.dot(jnp.concatenate(probs, axis=1), vw_ref[0], preferred_element_type=F32)
    merged = merged + gt_ref[0, :, 2 * D_MODEL:].astype(F32) * y_mem

    y = jnp.dot(merged.astype(BF16), w_o_ref[...], preferred_element_type=F32)
    o_ref[0] = x_ref[0] + _rms(y, g_ref[...])


def _mix_out(dec, x, q, k, v, sg, st_f, st_b, ret_gn, p, mq, gates, k_mem, vw_mem,
             w_ret, w_pool, w_o, g_post):
    batch, seq, _ = x.shape
    tm = ROW_TILE
    assert tm == RET_CHUNK
    halo = BF16_SUBLANES
    n_tiles = seq // tm
    halo_per_tile = tm // halo
    n_halo_blocks = seq // halo
    row_spec = pl.BlockSpec((1, tm, D_MODEL), lambda b, t: (b, t, 0))
    prev_spec = pl.BlockSpec((1, halo, D_MODEL),
                             lambda b, t: (b, jnp.maximum(t * halo_per_tile - 1, 0), 0))
    next_spec = pl.BlockSpec((1, halo, D_MODEL),
                             lambda b, t: (b, jnp.minimum((t + 1) * halo_per_tile, n_halo_blocks - 1), 0))
    state_spec = pl.BlockSpec((1, D_MODEL, HEAD_W), lambda b, t: (b * n_tiles + t, 0, 0))
    return pl.pallas_call(
        functools.partial(_mix_out_kernel, seq=seq),
        out_shape=jax.ShapeDtypeStruct(x.shape, F32),
        grid=(batch, n_tiles),
        in_specs=[
            pl.BlockSpec(memory_space=pltpu.SMEM),
            row_spec, row_spec, row_spec, row_spec, row_spec, state_spec, state_spec,
            _const_spec(ret_gn.shape),
            row_spec, prev_spec, next_spec, row_spec,
            pl.BlockSpec((1, tm, N_BRANCH * D_MODEL), lambda b, t: (b, t, 0)),
            pl.BlockSpec((1, MEM_TOKENS, D_MODEL), lambda b, t: (b, 0, 0), pipeline_mode=pl.Buffered(1)),
            pl.BlockSpec((1, N_HEADS * MEM_TOKENS, D_MODEL), lambda b, t: (b, 0, 0),
                         pipeline_mode=pl.Buffered(1)),
            _const_spec(w_ret.shape), _const_spec(w_pool.shape), _const_spec(w_o.shape),
            _const_spec(g_post.shape),
        ],
        out_specs=row_spec,
        scratch_shapes=[
            pltpu.VMEM((D_MODEL // LANES, tm + 2 * halo, LANES), F32),
            pltpu.VMEM((2, tm + 2 * halo, LANES), F32),
            pltpu.VMEM((N_HEADS, RET_CHUNK, RET_CHUNK), F32),
            pltpu.VMEM((2, N_HEADS, RET_CHUNK, LANES), F32),
        ],
        compiler_params=pltpu.CompilerParams(
            dimension_semantics=("arbitrary", "arbitrary"), vmem_limit_bytes=VMEM_LIMIT_BYTES),
        name="mix_out",
    )(dec, x, q, k, v, sg, st_f, st_b, ret_gn, p, p, p, mq, gates, k_mem, vw_mem,
      w_ret, w_pool, w_o, g_post)


def _gelu_tanh(x):
    return 0.5 * x * (1.0 + jnp.tanh(0.7978845608028654 * (x + 0.044715 * (x * x * x))))


def _ffn_kernel(x_ref, xprev_ref, xnext_ref, g_pre_ref, w_up_ref, conv_w_ref, conv_b_ref,
                w_down_ref, g_post_ref, o_ref, u_ref):
    tm = ROW_TILE
    halo = F32_SUBLANES
    t = pl.program_id(1)
    has_prev = (t > 0).astype(F32)
    has_next = (t < pl.num_programs(1) - 1).astype(F32)
    x = x_ref[0]
    xext = jnp.concatenate([xprev_ref[0] * has_prev, x, xnext_ref[0] * has_next], axis=0)
    hext = _rms(xext, g_pre_ref[...]).astype(BF16)

    n_chunks = D_FF // FF_CHUNK
    lane_tiles = FF_CHUNK // LANES

    def up(j):
        for half in range(2):
            col0 = half * D_FF + j * FF_CHUNK
            u = jnp.dot(hext, w_up_ref[:, col0:col0 + FF_CHUNK], preferred_element_type=F32)
            for ct in range(lane_tiles):
                u_ref[j % FF_BUFS, half, ct] = u[:, ct * LANES:(ct + 1) * LANES]

    def conv(j, half):
        outs = []
        for ct in range(lane_tiles):
            u = u_ref.at[j % FF_BUFS, half, ct]
            col0 = half * D_FF + j * FF_CHUNK + ct * LANES
            cols = slice(col0, col0 + LANES)
            outs.append(u[halo - 1:halo - 1 + tm, :] * conv_w_ref[0:1, cols]
                        + u[halo:halo + tm, :] * conv_w_ref[1:2, cols]
                        + u[halo + 1:halo + 1 + tm, :] * conv_w_ref[2:3, cols]
                        + conv_b_ref[:, cols])
        return jnp.concatenate(outs, axis=1)

    acc = None
    for j in range(min(FF_BUFS - 1, n_chunks)):
        up(j)
    for j in range(n_chunks):
        if j + FF_BUFS - 1 < n_chunks:
            up(j + FF_BUFS - 1)
        a = (_gelu_tanh(conv(j, 0)) * conv(j, 1)).astype(BF16)
        part = jnp.dot(a, w_down_ref[j * FF_CHUNK:(j + 1) * FF_CHUNK, :], preferred_element_type=F32)
        acc = part if acc is None else acc + part
    o_ref[0] = x + _rms(acc, g_post_ref[...])


def _ffn(x, g_pre, w_up, conv_w, conv_b, w_down, g_post):
    batch, seq, _ = x.shape
    tm = ROW_TILE
    halo = F32_SUBLANES
    n_tiles = seq // tm
    halo_per_tile = tm // halo
    n_halo_blocks = seq // halo
    row_spec = pl.BlockSpec((1, tm, D_MODEL), lambda b, t: (b, t, 0))
    prev_spec = pl.BlockSpec((1, halo, D_MODEL),
                             lambda b, t: (b, jnp.maximum(t * halo_per_tile - 1, 0), 0))
    next_spec = pl.BlockSpec((1, halo, D_MODEL),
                             lambda b, t: (b, jnp.minimum((t + 1) * halo_per_tile, n_halo_blocks - 1), 0))
    return pl.pallas_call(
        _ffn_kernel,
        out_shape=jax.ShapeDtypeStruct(x.shape, F32),
        grid=(batch, n_tiles),
        in_specs=[
            row_spec, prev_spec, next_spec,
            _const_spec(g_pre.shape), _const_spec(w_up.shape), _const_spec(conv_w.shape),
            _const_spec(conv_b.shape), _const_spec(w_down.shape), _const_spec(g_post.shape),
        ],
        out_specs=row_spec,
        scratch_shapes=[pltpu.VMEM((FF_BUFS, 2, FF_CHUNK // LANES, tm + 2 * halo, LANES), F32)],
        compiler_params=pltpu.CompilerParams(
            dimension_semantics=("parallel", "parallel"), vmem_limit_bytes=VMEM_LIMIT_BYTES),
        name="ffn",
    )(x, x, x, g_pre, w_up, conv_w, conv_b, w_down, g_post)


def _rope_tables(seq):
    inv = ROPE_BASE ** (-jnp.arange(HALF_W, dtype=F32) / HALF_W)
    ang = jnp.arange(seq, dtype=F32)[:, None] * inv[None, :]
    return jnp.cos(ang), jnp.sin(ang)


def _layer(x, mem, w):
    batch, seq, _ = x.shape
    cos, sin = _rope_tables(seq)
    k_mem, vw_mem = _mem_kv(mem, w["g_mem"], w["w_mem_kv"], w["w_mem_out"])
    q, k, v, sg, p, mq, gates, kv = _in_proj(
        w["decay"], x.reshape(batch * seq, D_MODEL), w["g_mix_pre"], w["w_in"], cos, sin, seq)
    shp = (batch, seq, D_MODEL)
    q, k, v, sg = (a.reshape(shp) for a in (q, k, v, sg))
    st_f, st_b = _ret_state(w["decay"], kv, batch)
    x1 = _mix_out(w["decay"], x, q, k, v, sg, st_f, st_b, w["ret_gn"], p.reshape(shp), mq.reshape(shp),
                  gates.reshape(batch, seq, N_BRANCH * D_MODEL), k_mem, vw_mem,
                  w["w_ret_out"], w["w_pool_fold"], w["w_o"], w["g_mix_post"])
    return _ffn(x1, w["g_ffn_pre"], w["w_up"], w["conv_w"], w["conv_b"], w["w_down"], w["g_ffn_post"])


def kernel(x_prompt, x_sample, mem_prompt, mem_sample, g_mix_pre, g_mix_post, g_mem, w_in,
           decay_fwd, decay_bwd, ret_gn, w_ret_out, pool_w, pool_scale, w_pool_out,
           w_mem_kv, w_mem_out, w_o, g_ffn_pre, g_ffn_post, w_up, conv_w, conv_b, w_down):
    depth = w_in.shape[0]
    for l in range(depth):
        w = {
            "g_mix_pre": g_mix_pre[l][None], "g_mix_post": g_mix_post[l][None], "g_mem": g_mem[l][None],
            "w_in": w_in[l].astype(BF16),
            "decay": jnp.stack([decay_fwd[l], decay_bwd[l]]).astype(F32),
            "ret_gn": ret_gn[l][None],
            "w_ret_out": w_ret_out[l].astype(BF16),
            "w_pool_fold": _pool_fold(pool_w[l], pool_scale[l][None], w_pool_out[l]),
            "w_mem_kv": w_mem_kv[l].astype(BF16), "w_mem_out": w_mem_out[l].astype(BF16),
            "w_o": w_o[l].astype(BF16),
            "g_ffn_pre": g_ffn_pre[l][None], "g_ffn_post": g_ffn_post[l][None],
            "w_up": w_up[l].astype(BF16), "conv_w": conv_w[l], "conv_b": conv_b[l][None],
            "w_down": w_down[l].astype(BF16),
        }
        x_prompt = _layer(x_prompt, mem_prompt, w)
        x_sample = _layer(x_sample, mem_sample, w)
    return (x_prompt, x_sample)
```

```python
import functools

import jax
import jax.numpy as jnp
from jax import lax
from jax.experimental import pallas as pl
from jax.experimental.pallas import tpu as pltpu

F32 = jnp.float32
BF16 = jnp.bfloat16

D_MODEL = 1024
N_HEADS = 4
HEAD_W = 256
HALF_W = HEAD_W // 2
RET_CHUNK = 512
ROPE_BASE = 10000.0
POOL_WINDOWS = (2, 4, 8, 16)
MEM_TOKENS = 256
N_BRANCH = 3
D_FF = 2816
EPS = 1e-6
QK_SCALE = HEAD_W ** -0.5

VMEM_LIMIT_BYTES = 58 * 1024 * 1024
LANES = 128
BF16_SUBLANES = 16
F32_SUBLANES = 8

ROW_TILE = 512
FF_CHUNK = 256
FF_BUFS = 11
STATE_GROUP = 4


def _rms(x, g):
    ms = jnp.mean(x * x, axis=-1, keepdims=True)
    return x * lax.rsqrt(ms + EPS) * g


def _sigmoid(x):
    return 1.0 / (1.0 + jnp.exp(-x))


def _log_sigmoid(x):
    return -(jnp.maximum(-x, 0.0) + jnp.log1p(jnp.exp(-jnp.abs(x))))


def _head_cols(hd):
    return slice(hd * HEAD_W, (hd + 1) * HEAD_W)


def _const_spec(shape):
    zeros = (0,) * len(shape)
    return pl.BlockSpec(shape, lambda *_: zeros, pipeline_mode=pl.Buffered(1))


def _mem_kv_kernel(m_ref, g_ref, w_kv_ref, w_out_ref, k_ref, vw_ref):
    h = _rms(m_ref[0], g_ref[...]).astype(BF16)
    kv = jnp.dot(h, w_kv_ref[...], preferred_element_type=F32)
    k_ref[0] = kv[:, :D_MODEL].astype(BF16)
    for hd in range(N_HEADS):
        cols = _head_cols(hd)
        vh = kv[:, D_MODEL + hd * HEAD_W:D_MODEL + (hd + 1) * HEAD_W].astype(BF16)
        vw_ref[0, cols, :] = jnp.dot(vh, w_out_ref[cols, :], preferred_element_type=F32).astype(BF16)


def _mem_kv(mem, g_mem, w_kv, w_mem_out):
    batch = mem.shape[0]
    return pl.pallas_call(
        _mem_kv_kernel,
        out_shape=(jax.ShapeDtypeStruct((batch, MEM_TOKENS, D_MODEL), BF16),
                   jax.ShapeDtypeStruct((batch, N_HEADS * MEM_TOKENS, D_MODEL), BF16)),
        grid=(batch,),
        in_specs=[
            pl.BlockSpec((1, MEM_TOKENS, D_MODEL), lambda i: (i, 0, 0)),
            _const_spec((1, D_MODEL)),
            _const_spec(w_kv.shape),
            _const_spec(w_mem_out.shape),
        ],
        out_specs=(pl.BlockSpec((1, MEM_TOKENS, D_MODEL), lambda i: (i, 0, 0)),
                   pl.BlockSpec((1, N_HEADS * MEM_TOKENS, D_MODEL), lambda i: (i, 0, 0))),
        compiler_params=pltpu.CompilerParams(
            dimension_semantics=("parallel",), vmem_limit_bytes=VMEM_LIMIT_BYTES),
        name="mem_kv",
    )(mem, g_mem, w_kv, w_mem_out)


def _pool_fold_kernel(pw_ref, scale_ref, w_out_ref, o_ref):
    a = pw_ref[0] * scale_ref[...]
    o_ref[...] = jnp.dot(a, w_out_ref[...], preferred_element_type=F32,
                         precision=lax.Precision.HIGHEST).astype(BF16)


def _pool_fold(pool_w, pool_scale, w_pool_out):
    return pl.pallas_call(
        _pool_fold_kernel,
        out_shape=jax.ShapeDtypeStruct((D_MODEL, D_MODEL), BF16),
        grid=(N_HEADS,),
        in_specs=[
            pl.BlockSpec((1, HEAD_W, HEAD_W), lambda g: (g, 0, 0)),
            pl.BlockSpec((1, HEAD_W), lambda g: (0, g)),
            pl.BlockSpec((HEAD_W, D_MODEL), lambda g: (g, 0)),
        ],
        out_specs=pl.BlockSpec((HEAD_W, D_MODEL), lambda g: (g, 0)),
        compiler_params=pltpu.CompilerParams(
            dimension_semantics=("parallel",), vmem_limit_bytes=VMEM_LIMIT_BYTES),
        name="pool_fold",
    )(pool_w, pool_scale, w_pool_out)


def _in_proj_kernel(dec_ref, x_ref, g_ref, w_ref, cos_ref, sin_ref,
                    q_ref, k_ref, v_ref, sg_ref, p_ref, mq_ref, gt_ref, kv_ref, kdec_ref):
    c = RET_CHUNK

    @pl.when(pl.program_id(0) == 0)
    def _init_key_decay():
        idx = lax.broadcasted_iota(jnp.int32, (F32_SUBLANES, c), 1).astype(F32)
        for hd in range(N_HEADS):
            kdec_ref[0, hd] = jnp.exp(
                _log_sigmoid(jnp.full((F32_SUBLANES, c), dec_ref[0, hd], F32)) * (c - 1.0 - idx))
            kdec_ref[1, hd] = jnp.exp(
                _log_sigmoid(jnp.full((F32_SUBLANES, c), dec_ref[1, hd], F32)) * idx)

    h = _rms(x_ref[...], g_ref[...]).astype(BF16)
    cos = cos_ref[...]
    sin = sin_ref[...]

    def seg(i):
        return jnp.dot(h, w_ref[:, i * D_MODEL:(i + 1) * D_MODEL], preferred_element_type=F32)

    def rotary(z, hd, scale):
        lo = hd * HEAD_W
        x1 = z[:, lo:lo + HALF_W]
        x2 = z[:, lo + HALF_W:lo + HEAD_W]
        return jnp.concatenate([(x1 * cos - x2 * sin) * scale, (x1 * sin + x2 * cos) * scale], axis=1)

    for b in range(N_BRANCH):
        gt_ref[:, b * D_MODEL:(b + 1) * D_MODEL] = _sigmoid(seg(6 + b)).astype(BF16)
    rg = seg(3)
    sg_ref[...] = (rg * _sigmoid(rg)).astype(BF16)
    v = seg(2).astype(BF16)
    v_ref[...] = v
    zk = seg(1)
    for hd in range(N_HEADS):
        cols = _head_cols(hd)
        kh = rotary(zk, hd, QK_SCALE)
        k_ref[:, cols] = kh.astype(BF16)
        kt = kh.T
        for dr in range(2):
            kd = (kt * kdec_ref[dr, hd, 0:1, :]).astype(BF16)
            kv_ref[0, dr, cols, :] = jnp.dot(kd, v[:, cols], preferred_element_type=F32).astype(BF16)
    zq = seg(0)
    for hd in range(N_HEADS):
        q_ref[:, _head_cols(hd)] = rotary(zq, hd, 1.0).astype(BF16)
    p_ref[...] = seg(4).astype(BF16)
    mq_ref[...] = seg(5).astype(BF16)


def _in_proj(dec, x2d, g, w_in, cos, sin, seq):
    rows = x2d.shape[0]
    tm = ROW_TILE
    assert tm == RET_CHUNK
    tiles_per_seq = seq // tm
    n_tiles = rows // tm
    row_spec = pl.BlockSpec((tm, D_MODEL), lambda i: (i, 0))
    tab_spec = pl.BlockSpec((tm, HALF_W), lambda i: (i % tiles_per_seq, 0))
    out1 = jax.ShapeDtypeStruct((rows, D_MODEL), BF16)
    return pl.pallas_call(
        _in_proj_kernel,
        out_shape=(out1,) * 6 + (jax.ShapeDtypeStruct((rows, N_BRANCH * D_MODEL), BF16),
                                 jax.ShapeDtypeStruct((n_tiles, 2, D_MODEL, HEAD_W), BF16)),
        grid=(n_tiles,),
        in_specs=[pl.BlockSpec(memory_space=pltpu.SMEM),
                  row_spec, _const_spec((1, D_MODEL)), _const_spec(w_in.shape), tab_spec, tab_spec],
        out_specs=(row_spec,) * 6 + (pl.BlockSpec((tm, N_BRANCH * D_MODEL), lambda i: (i, 0)),
                                     pl.BlockSpec((1, 2, D_MODEL, HEAD_W), lambda i: (i, 0, 0, 0))),
        scratch_shapes=[pltpu.VMEM((2, N_HEADS, F32_SUBLANES, RET_CHUNK), F32)],
        compiler_params=pltpu.CompilerParams(
            dimension_semantics=("arbitrary",), vmem_limit_bytes=VMEM_LIMIT_BYTES),
        name="in_proj",
    )(dec, x2d, g, w_in, cos, sin)


def _ret_state_kernel(dec_ref, kvf_ref, kvb_ref, of_ref, ob_ref, sf_ref, sb_ref, cdec_ref):
    group = kvf_ref.shape[0]

    @pl.when(pl.program_id(1) == 0)
    def _init_sequence():
        sf_ref[...] = jnp.zeros_like(sf_ref)
        sb_ref[...] = jnp.zeros_like(sb_ref)
        for dr in range(2):
            for hd in range(N_HEADS):
                cdec_ref[dr, hd] = jnp.exp(
                    _log_sigmoid(jnp.full((F32_SUBLANES, HEAD_W), dec_ref[dr, hd], F32)) * RET_CHUNK)

    def step(dr, state_ref, kv_ref, o_ref, j):
        for hd in range(N_HEADS):
            cols = _head_cols(hd)
            state = state_ref[cols, :]
            o_ref[j, cols, :] = state.astype(BF16)
            state_ref[cols, :] = state * cdec_ref[dr, hd, 0:1, :] + kv_ref[j, 0, cols, :].astype(F32)

    for j in range(group):
        step(0, sf_ref, kvf_ref, of_ref, j)
    for j in reversed(range(group)):
        step(1, sb_ref, kvb_ref, ob_ref, j)


def _ret_state(dec, kv, batch):
    n_chunks = kv.shape[0] // batch
    group = min(STATE_GROUP, n_chunks)
    n_groups = n_chunks // group
    out = jax.ShapeDtypeStruct((batch * n_chunks, D_MODEL, HEAD_W), BF16)
    return pl.pallas_call(
        _ret_state_kernel,
        out_shape=(out, out),
        grid=(batch, n_groups),
        in_specs=[
            pl.BlockSpec(memory_space=pltpu.SMEM),
            pl.BlockSpec((group, 1, D_MODEL, HEAD_W), lambda b, t: (b * n_groups + t, 0, 0, 0)),
            pl.BlockSpec((group, 1, D_MODEL, HEAD_W),
                         lambda b, t: (b * n_groups + n_groups - 1 - t, 1, 0, 0)),
        ],
        out_specs=(
            pl.BlockSpec((group, D_MODEL, HEAD_W), lambda b, t: (b * n_groups + t, 0, 0)),
            pl.BlockSpec((group, D_MODEL, HEAD_W), lambda b, t: (b * n_groups + n_groups - 1 - t, 0, 0)),
        ),
        scratch_shapes=[
            pltpu.VMEM((D_MODEL, HEAD_W), F32),
            pltpu.VMEM((D_MODEL, HEAD_W), F32),
            pltpu.VMEM((2, N_HEADS, F32_SUBLANES, HEAD_W), F32),
        ],
        compiler_params=pltpu.CompilerParams(
            dimension_semantics=("arbitrary", "arbitrary"), vmem_limit_bytes=VMEM_LIMIT_BYTES),
        name="ret_state",
    )(dec, kv, kv)


def _mix_out_kernel(dec_ref, x_ref, q_ref, k_ref, v_ref, sg_ref, stf_ref, stb_ref, gn_ref,
                    p_ref, pprev_ref, pnext_ref, mq_ref, gt_ref, km_ref, vw_ref,
                    w_ret_ref, w_pool_ref, w_o_ref, g_ref, o_ref,
                    pext_ref, psum_ref, dmask_ref, qdec_ref, *, seq):
    tm = ROW_TILE
    c = RET_CHUNK
    halo = BF16_SUBLANES
    t = pl.program_id(1)
    has_prev = (t > 0).astype(F32)
    has_next = (t < pl.num_programs(1) - 1).astype(F32)

    @pl.when(jnp.logical_and(pl.program_id(0) == 0, t == 0))
    def _init_decay():
        diff = (lax.broadcasted_iota(jnp.int32, (c, c), 0)
                - lax.broadcasted_iota(jnp.int32, (c, c), 1)).astype(F32)
        idx = lax.broadcasted_iota(jnp.int32, (c, LANES), 0).astype(F32)
        for hd in range(N_HEADS):
            lg_f = _log_sigmoid(jnp.full((c, c), dec_ref[0, hd], F32))
            lg_b = _log_sigmoid(jnp.full((c, c), dec_ref[1, hd], F32))
            dmask_ref[hd] = jnp.where(diff >= 0, jnp.exp(lg_f * jnp.maximum(diff, 0.0)),
                                      jnp.exp(lg_b * jnp.maximum(-diff, 0.0)))
            qdec_ref[0, hd] = jnp.exp(_log_sigmoid(jnp.full((c, LANES), dec_ref[0, hd], F32)) * (idx + 1.0))
            qdec_ref[1, hd] = jnp.exp(_log_sigmoid(jnp.full((c, LANES), dec_ref[1, hd], F32)) * (c - idx))

    def cross(q, st_ref, dr, hd):
        y = jnp.dot(q, st_ref[0, _head_cols(hd), :], preferred_element_type=F32)
        qd = qdec_ref[dr, hd]
        return jnp.concatenate([y[:, :LANES] * qd, y[:, LANES:] * qd], axis=1)

    yr = []
    for hd in range(N_HEADS):
        cols = _head_cols(hd)
        q = q_ref[0, :, cols]
        s = lax.dot_general(q, k_ref[0, :, cols], (((1,), (1,)), ((), ())),
                            preferred_element_type=F32)
        pm = (s * dmask_ref[hd]).astype(BF16)
        y = (jnp.dot(pm, v_ref[0, :, cols], preferred_element_type=F32)
             + cross(q, stf_ref, 0, hd) + cross(q, stb_ref, 1, hd))
        mu = jnp.sum(y[:, :LANES] + y[:, LANES:], axis=-1, keepdims=True) * (1.0 / HEAD_W)
        yc = y - mu
        sq = yc * yc
        var = jnp.sum(sq[:, :LANES] + sq[:, LANES:], axis=-1, keepdims=True) * (1.0 / HEAD_W)
        yn = yc * lax.rsqrt(var + EPS) * gn_ref[:, cols]
        yr.append((sg_ref[0, :, cols].astype(F32) * yn).astype(BF16))
    merged = gt_ref[0, :, 0:D_MODEL].astype(F32) * jnp.dot(
        jnp.concatenate(yr, axis=1), w_ret_ref[...], preferred_element_type=F32)

    pprev = pprev_ref[0].astype(F32) * has_prev
    pmain = p_ref[0].astype(F32)
    pnext = pnext_ref[0].astype(F32) * has_next
    for ct in range(D_MODEL // LANES):
        lanes = slice(ct * LANES, (ct + 1) * LANES)
        pext_ref[ct, 0:halo, :] = pprev[:, lanes]
        pext_ref[ct, halo:halo + tm, :] = pmain[:, lanes]
        pext_ref[ct, halo + tm:, :] = pnext[:, lanes]
    pos = t * tm + lax.broadcasted_iota(jnp.int32, (tm, LANES), 0)
    d = []
    for gi, w in enumerate(POOL_WINDOWS):
        cnt = (jnp.minimum(pos + w // 2, seq) - jnp.maximum(pos - w // 2, 0)).astype(F32)
        inv_cnt = 1.0 / cnt
        for ct in range(gi * HEAD_W // LANES, (gi + 1) * HEAD_W // LANES):
            start = halo - w // 2
            rows = tm + w - 2
            acc = pext_ref[ct, start:start + rows, :] + pext_ref[ct, start + 1:start + 1 + rows, :]
            span = 2
            while span < w:
                psum_ref[ct % 2, 0:rows, :] = acc
                rows -= span
                acc = psum_ref[ct % 2, 0:rows, :] + psum_ref[ct % 2, span:span + rows, :]
                span *= 2
            d.append(acc * inv_cnt - pext_ref[ct, halo:halo + tm, :])
    d = jnp.concatenate(d, axis=1).astype(BF16)
    y_pool = jnp.dot(d, w_pool_ref[...], preferred_element_type=F32)
    merged = merged + gt_ref[0, :, D_MODEL:2 * D_MODEL].astype(F32) * y_pool

    probs = []
    for hd in range(N_HEADS):
        cols = _head_cols(hd)
        s = lax.dot_general(mq_ref[0, :, cols], km_ref[0, :, cols], (((1,), (1,)), ((), ())),
                            preferred_element_type=F32)
        e = jnp.exp((s - jnp.max(s, axis=-1, keepdims=True)) * QK_SCALE)
        probs.append((e * (1.0 / jnp.sum(e, axis=-1, keepdims=True))).astype(BF16))
    y_mem = jnp.dot(jnp.concatenate(probs, axis=1), vw_ref[0], preferred_element_type=F32)
    merged = merged + gt_ref[0, :, 2 * D_MODEL:].astype(F32) * y_mem

    y = jnp.dot(merged.astype(BF16), w_o_ref[...], preferred_element_type=F32)
    o_ref[0] = x_ref[0] + _rms(y, g_ref[...])


def _mix_out(dec, x, q, k, v, sg, st_f, st_b, ret_gn, p, mq, gates, k_mem, vw_mem,
             w_ret, w_pool, w_o, g_post):
    batch, seq, _ = x.shape
    tm = ROW_TILE
    assert tm == RET_CHUNK
    halo = BF16_SUBLANES
    n_tiles = seq // tm
    halo_per_tile = tm // halo
    n_halo_blocks = seq // halo
    row_spec = pl.BlockSpec((1, tm, D_MODEL), lambda b, t: (b, t, 0))
    prev_spec = pl.BlockSpec((1, halo, D_MODEL),
                             lambda b, t: (b, jnp.maximum(t * halo_per_tile - 1, 0), 0))
    next_spec = pl.BlockSpec((1, halo, D_MODEL),
                             lambda b, t: (b, jnp.minimum((t + 1) * halo_per_tile, n_halo_blocks - 1), 0))
    state_spec = pl.BlockSpec((1, D_MODEL, HEAD_W), lambda b, t: (b * n_tiles + t, 0, 0))
    return pl.pallas_call(
        functools.partial(_mix_out_kernel, seq=seq),
        out_shape=jax.ShapeDtypeStruct(x.shape, F32),
        grid=(batch, n_tiles),
        in_specs=[
            pl.BlockSpec(memory_space=pltpu.SMEM),
            row_spec, row_spec, row_spec, row_spec, row_spec, state_spec, state_spec,
            _const_spec(ret_gn.shape),
            row_spec, prev_spec, next_spec, row_spec,
            pl.BlockSpec((1, tm, N_BRANCH * D_MODEL), lambda b, t: (b, t, 0)),
            pl.BlockSpec((1, MEM_TOKENS, D_MODEL), lambda b, t: (b, 0, 0), pipeline_mode=pl.Buffered(1)),
            pl.BlockSpec((1, N_HEADS * MEM_TOKENS, D_MODEL), lambda b, t: (b, 0, 0),
                         pipeline_mode=pl.Buffered(1)),
            _const_spec(w_ret.shape), _const_spec(w_pool.shape), _const_spec(w_o.shape),
            _const_spec(g_post.shape),
        ],
        out_specs=row_spec,
        scratch_shapes=[
            pltpu.VMEM((D_MODEL // LANES, tm + 2 * halo, LANES), F32),
            pltpu.VMEM((2, tm + 2 * halo, LANES), F32),
            pltpu.VMEM((N_HEADS, RET_CHUNK, RET_CHUNK), F32),
            pltpu.VMEM((2, N_HEADS, RET_CHUNK, LANES), F32),
        ],
        compiler_params=pltpu.CompilerParams(
            dimension_semantics=("arbitrary", "arbitrary"), vmem_limit_bytes=VMEM_LIMIT_BYTES),
        name="mix_out",
    )(dec, x, q, k, v, sg, st_f, st_b, ret_gn, p, p, p, mq, gates, k_mem, vw_mem,
      w_ret, w_pool, w_o, g_post)


def _gelu_tanh(x):
    return 0.5 * x * (1.0 + jnp.tanh(0.7978845608028654 * (x + 0.044715 * (x * x * x))))


def _ffn_kernel(x_ref, xprev_ref, xnext_ref, g_pre_ref, w_up_ref, conv_w_ref, conv_b_ref,
                w_down_ref, g_post_ref, o_ref, u_ref):
    tm = ROW_TILE
    halo = F32_SUBLANES
    t = pl.program_id(1)
    has_prev = (t > 0).astype(F32)
    has_next = (t < pl.num_programs(1) - 1).astype(F32)
    x = x_ref[0]
    xext = jnp.concatenate([xprev_ref[0] * has_prev, x, xnext_ref[0] * has_next], axis=0)
    hext = _rms(xext, g_pre_ref[...]).astype(BF16)

    n_chunks = D_FF // FF_CHUNK
    lane_tiles = FF_CHUNK // LANES

    def up(j):
        for half in range(2):
            col0 = half * D_FF + j * FF_CHUNK
            u = jnp.dot(hext, w_up_ref[:, col0:col0 + FF_CHUNK], preferred_element_type=F32)
            for ct in range(lane_tiles):
                u_ref[j % FF_BUFS, half, ct] = u[:, ct * LANES:(ct + 1) * LANES]

    def conv(j, half):
        outs = []
        for ct in range(lane_tiles):
            u = u_ref.at[j % FF_BUFS, half, ct]
            col0 = half * D_FF + j * FF_CHUNK + ct * LANES
            cols = slice(col0, col0 + LANES)
            outs.append(u[halo - 1:halo - 1 + tm, :] * conv_w_ref[0:1, cols]
                        + u[halo:halo + tm, :] * conv_w_ref[1:2, cols]
                        + u[halo + 1:halo + 1 + tm, :] * conv_w_ref[2:3, cols]
                        + conv_b_ref[:, cols])
        return jnp.concatenate(outs, axis=1)

    acc = None
    for j in range(min(FF_BUFS - 1, n_chunks)):
        up(j)
    for j in range(n_chunks):
        if j + FF_BUFS - 1 < n_chunks:
            up(j + FF_BUFS - 1)
        a = (_gelu_tanh(conv(j, 0)) * conv(j, 1)).astype(BF16)
        part = jnp.dot(a, w_down_ref[j * FF_CHUNK:(j + 1) * FF_CHUNK, :], preferred_element_type=F32)
        acc = part if acc is None else acc + part
    o_ref[0] = x + _rms(acc, g_post_ref[...])


def _ffn(x, g_pre, w_up, conv_w, conv_b, w_down, g_post):
    batch, seq, _ = x.shape
    tm = ROW_TILE
    halo = F32_SUBLANES
    n_tiles = seq // tm
    halo_per_tile = tm // halo
    n_halo_blocks = seq // halo
    row_spec = pl.BlockSpec((1, tm, D_MODEL), lambda b, t: (b, t, 0))
    prev_spec = pl.BlockSpec((1, halo, D_MODEL),
                             lambda b, t: (b, jnp.maximum(t * halo_per_tile - 1, 0), 0))
    next_spec = pl.BlockSpec((1, halo, D_MODEL),
                             lambda b, t: (b, jnp.minimum((t + 1) * halo_per_tile, n_halo_blocks - 1), 0))
    return pl.pallas_call(
        _ffn_kernel,
        out_shape=jax.ShapeDtypeStruct(x.shape, F32),
        grid=(batch, n_tiles),
        in_specs=[
            row_spec, prev_spec, next_spec,
            _const_spec(g_pre.shape), _const_spec(w_up.shape), _const_spec(conv_w.shape),
            _const_spec(conv_b.shape), _const_spec(w_down.shape), _const_spec(g_post.shape),
        ],
        out_specs=row_spec,
        scratch_shapes=[pltpu.VMEM((FF_BUFS, 2, FF_CHUNK // LANES, tm + 2 * halo, LANES), F32)],
        compiler_params=pltpu.CompilerParams(
            dimension_semantics=("parallel", "parallel"), vmem_limit_bytes=VMEM_LIMIT_BYTES),
        name="ffn",
    )(x, x, x, g_pre, w_up, conv_w, conv_b, w_down, g_post)


def _rope_tables(seq):
    inv = ROPE_BASE ** (-jnp.arange(HALF_W, dtype=F32) / HALF_W)
    ang = jnp.arange(seq, dtype=F32)[:, None] * inv[None, :]
    return jnp.cos(ang), jnp.sin(ang)


def _layer(x, mem, w):
    batch, seq, _ = x.shape
    cos, sin = _rope_tables(seq)
    k_mem, vw_mem = _mem_kv(mem, w["g_mem"], w["w_mem_kv"], w["w_mem_out"])
    q, k, v, sg, p, mq, gates, kv = _in_proj(
        w["decay"], x.reshape(batch * seq, D_MODEL), w["g_mix_pre"], w["w_in"], cos, sin, seq)
    shp = (batch, seq, D_MODEL)
    q, k, v, sg = (a.reshape(shp) for a in (q, k, v, sg))
    st_f, st_b = _ret_state(w["decay"], kv, batch)
    x1 = _mix_out(w["decay"], x, q, k, v, sg, st_f, st_b, w["ret_gn"], p.reshape(shp), mq.reshape(shp),
                  gates.reshape(batch, seq, N_BRANCH * D_MODEL), k_mem, vw_mem,
                  w["w_ret_out"], w["w_pool_fold"], w["w_o"], w["g_mix_post"])
    return _ffn(x1, w["g_ffn_pre"], w["w_up"], w["conv_w"], w["conv_b"], w["w_down"], w["g_ffn_post"])


def kernel(x_prompt, x_sample, mem_prompt, mem_sample, g_mix_pre, g_mix_post, g_mem, w_in,
           decay_fwd, decay_bwd, ret_gn, w_ret_out, pool_w, pool_scale, w_pool_out,
           w_mem_kv, w_mem_out, w_o, g_ffn_pre, g_ffn_post, w_up, conv_w, conv_b, w_down):
    depth = w_in.shape[0]
    for l in range(depth):
        w = {
            "g_mix_pre": g_mix_pre[l][None], "g_mix_post": g_mix_post[l][None], "g_mem": g_mem[l][None],
            "w_in": w_in[l].astype(BF16),
            "decay": jnp.stack([decay_fwd[l], decay_bwd[l]]).astype(F32),
            "ret_gn": ret_gn[l][None],
            "w_ret_out": w_ret_out[l].astype(BF16),
            "w_pool_fold": _pool_fold(pool_w[l], pool_scale[l][None], w_pool_out[l]),
            "w_mem_kv": w_mem_kv[l].astype(BF16), "w_mem_out": w_mem_out[l].astype(BF16),
            "w_o": w_o[l].astype(BF16),
            "g_ffn_pre": g_ffn_pre[l][None], "g_ffn_post": g_ffn_post[l][None],
            "w_up": w_up[l].astype(BF16), "conv_w": conv_w[l], "conv_b": conv_b[l][None],
            "w_down": w_down[l].astype(BF16),
        }
        x_prompt = _layer(x_prompt, mem_prompt, w)
        x_sample = _layer(x_sample, mem_sample, w)
    return (x_prompt, x_sample)
```

```python
import functools

import jax
import jax.numpy as jnp
from jax import lax
from jax.experimental import pallas as pl
from jax.experimental.pallas import tpu as pltpu

F32 = jnp.float32
BF16 = jnp.bfloat16

D_MODEL = 1024
N_HEADS = 4
HEAD_W = 256
HALF_W = HEAD_W // 2
RET_CHUNK = 512
ROPE_BASE = 10000.0
POOL_WINDOWS = (2, 4, 8, 16)
MEM_TOKENS = 256
N_BRANCH = 3
D_FF = 2816
EPS = 1e-6
QK_SCALE = HEAD_W ** -0.5

VMEM_LIMIT_BYTES = 58 * 1024 * 1024
LANES = 128
BF16_SUBLANES = 16
F32_SUBLANES = 8

ROW_TILE = 512
FF_CHUNK = 256
FF_BUFS = 11
STATE_GROUP = 4


def _rms(x, g):
    ms = jnp.mean(x * x, axis=-1, keepdims=True)
    return x * lax.rsqrt(ms + EPS) * g


def _sigmoid(x):
    return 1.0 / (1.0 + jnp.exp(-x))


def _log_sigmoid(x):
    return -(jnp.maximum(-x, 0.0) + jnp.log1p(jnp.exp(-jnp.abs(x))))


def _head_cols(hd):
    return slice(hd * HEAD_W, (hd + 1) * HEAD_W)


def _const_spec(shape):
    zeros = (0,) * len(shape)
    return pl.BlockSpec(shape, lambda *_: zeros, pipeline_mode=pl.Buffered(1))


def _mem_kv_kernel(m_ref, g_ref, w_kv_ref, w_out_ref, k_ref, vw_ref):
    h = _rms(m_ref[0], g_ref[...]).astype(BF16)
    kv = jnp.dot(h, w_kv_ref[...], preferred_element_type=F32)
    k_ref[0] = kv[:, :D_MODEL].astype(BF16)
    for hd in range(N_HEADS):
        cols = _head_cols(hd)
        vh = kv[:, D_MODEL + hd * HEAD_W:D_MODEL + (hd + 1) * HEAD_W].astype(BF16)
        vw_ref[0, cols, :] = jnp.dot(vh, w_out_ref[cols, :], preferred_element_type=F32).astype(BF16)


def _mem_kv(mem, g_mem, w_kv, w_mem_out):
    batch = mem.shape[0]
    return pl.pallas_call(
        _mem_kv_kernel,
        out_shape=(jax.ShapeDtypeStruct((batch, MEM_TOKENS, D_MODEL), BF16),
                   jax.ShapeDtypeStruct((batch, N_HEADS * MEM_TOKENS, D_MODEL), BF16)),
        grid=(batch,),
        in_specs=[
            pl.BlockSpec((1, MEM_TOKENS, D_MODEL), lambda i: (i, 0, 0)),
            _const_spec((1, D_MODEL)),
            _const_spec(w_kv.shape),
            _const_spec(w_mem_out.shape),
        ],
        out_specs=(pl.BlockSpec((1, MEM_TOKENS, D_MODEL), lambda i: (i, 0, 0)),
                   pl.BlockSpec((1, N_HEADS * MEM_TOKENS, D_MODEL), lambda i: (i, 0, 0))),
        compiler_params=pltpu.CompilerParams(
            dimension_semantics=("parallel",), vmem_limit_bytes=VMEM_LIMIT_BYTES),
        name="mem_kv",
    )(mem, g_mem, w_kv, w_mem_out)


def _pool_fold_kernel(pw_ref, scale_ref, w_out_ref, o_ref):
    a = pw_ref[0] * scale_ref[...]
    o_ref[...] = jnp.dot(a, w_out_ref[...], preferred_element_type=F32,
                         precision=lax.Precision.HIGHEST).astype(BF16)


def _pool_fold(pool_w, pool_scale, w_pool_out):
    return pl.pallas_call(
        _pool_fold_kernel,
        out_shape=jax.ShapeDtypeStruct((D_MODEL, D_MODEL), BF16),
        grid=(N_HEADS,),
        in_specs=[
            pl.BlockSpec((1, HEAD_W, HEAD_W), lambda g: (g, 0, 0)),
            pl.BlockSpec((1, HEAD_W), lambda g: (0, g)),
            pl.BlockSpec((HEAD_W, D_MODEL), lambda g: (g, 0)),
        ],
        out_specs=pl.BlockSpec((HEAD_W, D_MODEL), lambda g: (g, 0)),
        compiler_params=pltpu.CompilerParams(
            dimension_semantics=("parallel",), vmem_limit_bytes=VMEM_LIMIT_BYTES),
        name="pool_fold",
    )(pool_w, pool_scale, w_pool_out)


def _in_proj_kernel(dec_ref, x_ref, g_ref, w_ref, cos_ref, sin_ref,
                    q_ref, k_ref, v_ref, sg_ref, p_ref, mq_ref, gt_ref, kv_ref, kdec_ref):
    c = RET_CHUNK

    @pl.when(pl.program_id(0) == 0)
    def _init_key_decay():
        idx = lax.broadcasted_iota(jnp.int32, (F32_SUBLANES, c), 1).astype(F32)
        for hd in range(N_HEADS):
            kdec_ref[0, hd] = jnp.exp(
                _log_sigmoid(jnp.full((F32_SUBLANES, c), dec_ref[0, hd], F32)) * (c - 1.0 - idx))
            kdec_ref[1, hd] = jnp.exp(
                _log_sigmoid(jnp.full((F32_SUBLANES, c), dec_ref[1, hd], F32)) * idx)

    h = _rms(x_ref[...], g_ref[...]).astype(BF16)
    cos = cos_ref[...]
    sin = sin_ref[...]

    def seg(i):
        return jnp.dot(h, w_ref[:, i * D_MODEL:(i + 1) * D_MODEL], preferred_element_type=F32)

    def rotary(z, hd, scale):
        lo = hd * HEAD_W
        x1 = z[:, lo:lo + HALF_W]
        x2 = z[:, lo + HALF_W:lo + HEAD_W]
        return jnp.concatenate([(x1 * cos - x2 * sin) * scale, (x1 * sin + x2 * cos) * scale], axis=1)

    for b in range(N_BRANCH):
        gt_ref[:, b * D_MODEL:(b + 1) * D_MODEL] = _sigmoid(seg(6 + b)).astype(BF16)
    rg = seg(3)
    sg_ref[...] = (rg * _sigmoid(rg)).astype(BF16)
    v = seg(2).astype(BF16)
    v_ref[...] = v
    zk = seg(1)
    for hd in range(N_HEADS):
        cols = _head_cols(hd)
        kh = rotary(zk, hd, QK_SCALE)
        k_ref[:, cols] = kh.astype(BF16)
        kt = kh.T
        for dr in range(2):
            kd = (kt * kdec_ref[dr, hd, 0:1, :]).astype(BF16)
            kv_ref[0, dr, cols, :] = jnp.dot(kd, v[:, cols], preferred_element_type=F32).astype(BF16)
    zq = seg(0)
    for hd in range(N_HEADS):
        q_ref[:, _head_cols(hd)] = rotary(zq, hd, 1.0).astype(BF16)
    p_ref[...] = seg(4).astype(BF16)
    mq_ref[...] = seg(5).astype(BF16)


def _in_proj(dec, x2d, g, w_in, cos, sin, seq):
    rows = x2d.shape[0]
    tm = ROW_TILE
    assert tm == RET_CHUNK
    tiles_per_seq = seq // tm
    n_tiles = rows // tm
    row_spec = pl.BlockSpec((tm, D_MODEL), lambda i: (i, 0))
    tab_spec = pl.BlockSpec((tm, HALF_W), lambda i: (i % tiles_per_seq, 0))
    out1 = jax.ShapeDtypeStruct((rows, D_MODEL), BF16)
    return pl.pallas_call(
        _in_proj_kernel,
        out_shape=(out1,) * 6 + (jax.ShapeDtypeStruct((rows, N_BRANCH * D_MODEL), BF16),
                                 jax.ShapeDtypeStruct((n_tiles, 2, D_MODEL, HEAD_W), BF16)),
        grid=(n_tiles,),
        in_specs=[pl.BlockSpec(memory_space=pltpu.SMEM),
                  row_spec, _const_spec((1, D_MODEL)), _const_spec(w_in.shape), tab_spec, tab_spec],
        out_specs=(row_spec,) * 6 + (pl.BlockSpec((tm, N_BRANCH * D_MODEL), lambda i: (i, 0)),
                                     pl.BlockSpec((1, 2, D_MODEL, HEAD_W), lambda i: (i, 0, 0, 0))),
        scratch_shapes=[pltpu.VMEM((2, N_HEADS, F32_SUBLANES, RET_CHUNK), F32)],
        compiler_params=pltpu.CompilerParams(
            dimension_semantics=("arbitrary",), vmem_limit_bytes=VMEM_LIMIT_BYTES),
        name="in_proj",
    )(dec, x2d, g, w_in, cos, sin)


def _ret_state_kernel(dec_ref, kvb_ref, ob_ref, sb_ref, cdec_ref):
    group = kvb_ref.shape[0]

    @pl.when(pl.program_id(1) == 0)
    def _init_sequence():
        sb_ref[...] = jnp.zeros_like(sb_ref)
        for hd in range(N_HEADS):
            cdec_ref[hd] = jnp.exp(
                _log_sigmoid(jnp.full((F32_SUBLANES, HEAD_W), dec_ref[1, hd], F32)) * RET_CHUNK)

    for j in reversed(range(group)):
        for hd in range(N_HEADS):
            cols = _head_cols(hd)
            state = sb_ref[cols, :]
            ob_ref[j, cols, :] = state.astype(BF16)
            sb_ref[cols, :] = state * cdec_ref[hd, 0:1, :] + kvb_ref[j, 0, cols, :].astype(F32)


def _ret_state(dec, kv, batch):
    n_chunks = kv.shape[0] // batch
    group = min(STATE_GROUP, n_chunks)
    n_groups = n_chunks // group
    return pl.pallas_call(
        _ret_state_kernel,
        out_shape=jax.ShapeDtypeStruct((batch * n_chunks, D_MODEL, HEAD_W), BF16),
        grid=(batch, n_groups),
        in_specs=[
            pl.BlockSpec(memory_space=pltpu.SMEM),
            pl.BlockSpec((group, 1, D_MODEL, HEAD_W),
                         lambda b, t: (b * n_groups + n_groups - 1 - t, 1, 0, 0)),
        ],
        out_specs=pl.BlockSpec((group, D_MODEL, HEAD_W),
                               lambda b, t: (b * n_groups + n_groups - 1 - t, 0, 0)),
        scratch_shapes=[
            pltpu.VMEM((D_MODEL, HEAD_W), F32),
            pltpu.VMEM((N_HEADS, F32_SUBLANES, HEAD_W), F32),
        ],
        compiler_params=pltpu.CompilerParams(
            dimension_semantics=("arbitrary", "arbitrary"), vmem_limit_bytes=VMEM_LIMIT_BYTES),
        name="ret_state",
    )(dec, kv)


def _mix_out_kernel(dec_ref, x_ref, q_ref, k_ref, v_ref, sg_ref, kvf_ref, stb_ref, gn_ref,
                    p_ref, pprev_ref, pnext_ref, mq_ref, gt_ref, km_ref, vw_ref,
                    w_ret_ref, w_pool_ref, w_o_ref, g_ref, o_ref,
                    pext_ref, psum_ref, dmask_ref, qdec_ref, sf_ref, cdec_ref, *, seq):
    tm = ROW_TILE
    c = RET_CHUNK
    halo = BF16_SUBLANES
    t = pl.program_id(1)
    has_prev = (t > 0).astype(F32)
    has_next = (t < pl.num_programs(1) - 1).astype(F32)

    @pl.when(jnp.logical_and(pl.program_id(0) == 0, t == 0))
    def _init_decay():
        diff = (lax.broadcasted_iota(jnp.int32, (c, c), 0)
                - lax.broadcasted_iota(jnp.int32, (c, c), 1)).astype(F32)
        idx = lax.broadcasted_iota(jnp.int32, (c, LANES), 0).astype(F32)
        for hd in range(N_HEADS):
            lg_f = _log_sigmoid(jnp.full((c, c), dec_ref[0, hd], F32))
            lg_b = _log_sigmoid(jnp.full((c, c), dec_ref[1, hd], F32))
            dmask_ref[hd] = jnp.where(diff >= 0, jnp.exp(lg_f * jnp.maximum(diff, 0.0)),
                                      jnp.exp(lg_b * jnp.maximum(-diff, 0.0))).astype(BF16)
            qdec_ref[0, hd] = jnp.exp(_log_sigmoid(jnp.full((c, LANES), dec_ref[0, hd], F32)) * (idx + 1.0))
            qdec_ref[1, hd] = jnp.exp(_log_sigmoid(jnp.full((c, LANES), dec_ref[1, hd], F32)) * (c - idx))
            cdec_ref[hd] = jnp.exp(
                _log_sigmoid(jnp.full((F32_SUBLANES, HEAD_W), dec_ref[0, hd], F32)) * c)

    @pl.when(t == 0)
    def _init_state():
        sf_ref[...] = jnp.zeros_like(sf_ref)

    def cross(q, state, dr, hd):
        y = jnp.dot(q, state, preferred_element_type=F32)
        qd = qdec_ref[dr, hd]
        return jnp.concatenate([y[:, :LANES] * qd, y[:, LANES:] * qd], axis=1)

    yr = []
    for hd in range(N_HEADS):
        cols = _head_cols(hd)
        q = q_ref[0, :, cols]
        state_f = sf_ref[cols, :]
        s = lax.dot_general(q, k_ref[0, :, cols], (((1,), (1,)), ((), ())),
                            preferred_element_type=F32)
        pm = (s * dmask_ref[hd].astype(F32)).astype(BF16)
        y = (jnp.dot(pm, v_ref[0, :, cols], preferred_element_type=F32)
             + cross(q, state_f.astype(BF16), 0, hd) + cross(q, stb_ref[0, cols, :], 1, hd))
        sf_ref[cols, :] = state_f * cdec_ref[hd, 0:1, :] + kvf_ref[0, 0, cols, :].astype(F32)
        mu = jnp.sum(y[:, :LANES] + y[:, LANES:], axis=-1, keepdims=True) * (1.0 / HEAD_W)
        yc = y - mu
        sq = yc * yc
        var = jnp.sum(sq[:, :LANES] + sq[:, LANES:], axis=-1, keepdims=True) * (1.0 / HEAD_W)
        yn = yc * lax.rsqrt(var + EPS) * gn_ref[:, cols]
        yr.append((sg_ref[0, :, cols].astype(F32) * yn).astype(BF16))
    merged = gt_ref[0, :, 0:D_MODEL].astype(F32) * jnp.dot(
        jnp.concatenate(yr, axis=1), w_ret_ref[...], preferred_element_type=F32)

    pprev = pprev_ref[0].astype(F32) * has_prev
    pmain = p_ref[0].astype(F32)
    pnext = pnext_ref[0].astype(F32) * has_next
    for ct in range(D_MODEL // LANES):
        lanes = slice(ct * LANES, (ct + 1) * LANES)
        pext_ref[ct, 0:halo, :] = pprev[:, lanes]
        pext_ref[ct, halo:halo + tm, :] = pmain[:, lanes]
        pext_ref[ct, halo + tm:, :] = pnext[:, lanes]
    pos = t * tm + lax.broadcasted_iota(jnp.int32, (tm, LANES), 0)
    d = []
    for gi, w in enumerate(POOL_WINDOWS):
        cnt = (jnp.minimum(pos + w // 2, seq) - jnp.maximum(pos - w // 2, 0)).astype(F32)
        inv_cnt = 1.0 / cnt
        for ct in range(gi * HEAD_W // LANES, (gi + 1) * HEAD_W // LANES):
            start = halo - w // 2
            rows = tm + w - 2
            acc = pext_ref[ct, start:start + rows, :] + pext_ref[ct, start + 1:start + 1 + rows, :]
            span = 2
            while span < w:
                psum_ref[ct % 2, 0:rows, :] = acc
                rows -= span
                acc = psum_ref[ct % 2, 0:rows, :] + psum_ref[ct % 2, span:span + rows, :]
                span *= 2
            d.append(acc * inv_cnt - pext_ref[ct, halo:halo + tm, :])
    d = jnp.concatenate(d, axis=1).astype(BF16)
    y_pool = jnp.dot(d, w_pool_ref[...], preferred_element_type=F32)
    merged = merged + gt_ref[0, :, D_MODEL:2 * D_MODEL].astype(F32) * y_pool

    probs = []
    for hd in range(N_HEADS):
        cols = _head_cols(hd)
        s = lax.dot_general(mq_ref[0, :, cols], km_ref[0, :, cols], (((1,), (1,)), ((), ())),
                            preferred_element_type=F32)
        e = jnp.exp((s - jnp.max(s, axis=-1, keepdims=True)) * QK_SCALE)
        probs.append((e * (1.0 / jnp.sum(e, axis=-1, keepdims=True))).astype(BF16))
    y_mem = jnp.dot(jnp.concatenate(probs, axis=1), vw_ref[0], preferred_element_type=F32)
    merged = merged + gt_ref[0, :, 2 * D_MODEL:].astype(F32) * y_mem

    y = jnp.dot(merged.astype(BF16), w_o_ref[...], preferred_element_type=F32)
    o_ref[0] = x_ref[0] + _rms(y, g_ref[...])


def _mix_out(dec, x, q, k, v, sg, kv, st_b, ret_gn, p, mq, gates, k_mem, vw_mem,
             w_ret, w_pool, w_o, g_post):
    batch, seq, _ = x.shape
    tm = ROW_TILE
    assert tm == RET_CHUNK
    halo = BF16_SUBLANES
    n_tiles = seq // tm
    halo_per_tile = tm // halo
    n_halo_blocks = seq // halo
    row_spec = pl.BlockSpec((1, tm, D_MODEL), lambda b, t: (b, t, 0))
    prev_spec = pl.BlockSpec((1, halo, D_MODEL),
                             lambda b, t: (b, jnp.maximum(t * halo_per_tile - 1, 0), 0))
    next_spec = pl.BlockSpec((1, halo, D_MODEL),
                             lambda b, t: (b, jnp.minimum((t + 1) * halo_per_tile, n_halo_blocks - 1), 0))
    state_spec = pl.BlockSpec((1, D_MODEL, HEAD_W), lambda b, t: (b * n_tiles + t, 0, 0))
    return pl.pallas_call(
        functools.partial(_mix_out_kernel, seq=seq),
        out_shape=jax.ShapeDtypeStruct(x.shape, F32),
        grid=(batch, n_tiles),
        in_specs=[
            pl.BlockSpec(memory_space=pltpu.SMEM),
            row_spec, row_spec, row_spec, row_spec, row_spec,
            pl.BlockSpec((1, 1, D_MODEL, HEAD_W), lambda b, t: (b * n_tiles + t, 0, 0, 0)), state_spec,
            _const_spec(ret_gn.shape),
            row_spec, prev_spec, next_spec, row_spec,
            pl.BlockSpec((1, tm, N_BRANCH * D_MODEL), lambda b, t: (b, t, 0)),
            pl.BlockSpec((1, MEM_TOKENS, D_MODEL), lambda b, t: (b, 0, 0)),
            pl.BlockSpec((1, N_HEADS * MEM_TOKENS, D_MODEL), lambda b, t: (b, 0, 0)),
            _const_spec(w_ret.shape), _const_spec(w_pool.shape), _const_spec(w_o.shape),
            _const_spec(g_post.shape),
        ],
        out_specs=row_spec,
        scratch_shapes=[
            pltpu.VMEM((D_MODEL // LANES, tm + 2 * halo, LANES), F32),
            pltpu.VMEM((2, tm + 2 * halo, LANES), F32),
            pltpu.VMEM((N_HEADS, RET_CHUNK, RET_CHUNK), BF16),
            pltpu.VMEM((2, N_HEADS, RET_CHUNK, LANES), F32),
            pltpu.VMEM((D_MODEL, HEAD_W), F32),
            pltpu.VMEM((N_HEADS, F32_SUBLANES, HEAD_W), F32),
        ],
        compiler_params=pltpu.CompilerParams(
            dimension_semantics=("arbitrary", "arbitrary"), vmem_limit_bytes=VMEM_LIMIT_BYTES),
        name="mix_out",
    )(dec, x, q, k, v, sg, kv, st_b, ret_gn, p, p, p, mq, gates, k_mem, vw_mem,
      w_ret, w_pool, w_o, g_post)


def _gelu_tanh(x):
    return 0.5 * x * (1.0 + jnp.tanh(0.7978845608028654 * (x + 0.044715 * (x * x * x))))


def _ffn_kernel(x_ref, xprev_ref, xnext_ref, g_pre_ref, w_up_ref, conv_w_ref, conv_b_ref,
                w_down_ref, g_post_ref, o_ref, u_ref):
    tm = ROW_TILE
    halo = F32_SUBLANES
    t = pl.program_id(1)
    has_prev = (t > 0).astype(F32)
    has_next = (t < pl.num_programs(1) - 1).astype(F32)
    x = x_ref[0]
    xext = jnp.concatenate([xprev_ref[0] * has_prev, x, xnext_ref[0] * has_next], axis=0)
    hext = _rms(xext, g_pre_ref[...]).astype(BF16)

    n_chunks = D_FF // FF_CHUNK
    lane_tiles = FF_CHUNK // LANES

    def up(j):
        for half in range(2):
            col0 = half * D_FF + j * FF_CHUNK
            u = jnp.dot(hext, w_up_ref[:, col0:col0 + FF_CHUNK], preferred_element_type=F32)
            for ct in range(lane_tiles):
                u_ref[j % FF_BUFS, half, ct] = u[:, ct * LANES:(ct + 1) * LANES]

    def conv(j, half):
        outs = []
        for ct in range(lane_tiles):
            u = u_ref.at[j % FF_BUFS, half, ct]
            col0 = half * D_FF + j * FF_CHUNK + ct * LANES
            cols = slice(col0, col0 + LANES)
            outs.append(u[halo - 1:halo - 1 + tm, :] * conv_w_ref[0:1, cols]
                        + u[halo:halo + tm, :] * conv_w_ref[1:2, cols]
                        + u[halo + 1:halo + 1 + tm, :] * conv_w_ref[2:3, cols]
                        + conv_b_ref[:, cols])
        return jnp.concatenate(outs, axis=1)

    acc = None
    for j in range(min(FF_BUFS - 1, n_chunks)):
        up(j)
    for j in range(n_chunks):
        if j + FF_BUFS - 1 < n_chunks:
            up(j + FF_BUFS - 1)
        a = (_gelu_tanh(conv(j, 0)) * conv(j, 1)).astype(BF16)
        part = jnp.dot(a, w_down_ref[j * FF_CHUNK:(j + 1) * FF_CHUNK, :], preferred_element_type=F32)
        acc = part if acc is None else acc + part
    o_ref[0] = x + _rms(acc, g_post_ref[...])


def _ffn(x, g_pre, w_up, conv_w, conv_b, w_down, g_post):
    batch, seq, _ = x.shape
    tm = ROW_TILE
    halo = F32_SUBLANES
    n_tiles = seq // tm
    halo_per_tile = tm // halo
    n_halo_blocks = seq // halo
    row_spec = pl.BlockSpec((1, tm, D_MODEL), lambda b, t: (b, t, 0))
    prev_spec = pl.BlockSpec((1, halo, D_MODEL),
                             lambda b, t: (b, jnp.maximum(t * halo_per_tile - 1, 0), 0))
    next_spec = pl.BlockSpec((1, halo, D_MODEL),
                             lambda b, t: (b, jnp.minimum((t + 1) * halo_per_tile, n_halo_blocks - 1), 0))
    return pl.pallas_call(
        _ffn_kernel,
        out_shape=jax.ShapeDtypeStruct(x.shape, F32),
        grid=(batch, n_tiles),
        in_specs=[
            row_spec, prev_spec, next_spec,
            _const_spec(g_pre.shape), _const_spec(w_up.shape), _const_spec(conv_w.shape),
            _const_spec(conv_b.shape), _const_spec(w_down.shape), _const_spec(g_post.shape),
        ],
        out_specs=row_spec,
        scratch_shapes=[pltpu.VMEM((FF_BUFS, 2, FF_CHUNK // LANES, tm + 2 * halo, LANES), F32)],
        compiler_params=pltpu.CompilerParams(
            dimension_semantics=("parallel", "parallel"), vmem_limit_bytes=VMEM_LIMIT_BYTES),
        name="ffn",
    )(x, x, x, g_pre, w_up, conv_w, conv_b, w_down, g_post)


def _rope_tables(seq):
    inv = ROPE_BASE ** (-jnp.arange(HALF_W, dtype=F32) / HALF_W)
    ang = jnp.arange(seq, dtype=F32)[:, None] * inv[None, :]
    return jnp.cos(ang), jnp.sin(ang)


def _layer(x, mem, w):
    batch, seq, _ = x.shape
    cos, sin = _rope_tables(seq)
    k_mem, vw_mem = _mem_kv(mem, w["g_mem"], w["w_mem_kv"], w["w_mem_out"])
    q, k, v, sg, p, mq, gates, kv = _in_proj(
        w["decay"], x.reshape(batch * seq, D_MODEL), w["g_mix_pre"], w["w_in"], cos, sin, seq)
    shp = (batch, seq, D_MODEL)
    q, k, v, sg = (a.reshape(shp) for a in (q, k, v, sg))
    st_b = _ret_state(w["decay"], kv, batch)
    x1 = _mix_out(w["decay"], x, q, k, v, sg, kv, st_b, w["ret_gn"], p.reshape(shp), mq.reshape(shp),
                  gates.reshape(batch, seq, N_BRANCH * D_MODEL), k_mem, vw_mem,
                  w["w_ret_out"], w["w_pool_fold"], w["w_o"], w["g_mix_post"])
    return _ffn(x1, w["g_ffn_pre"], w["w_up"], w["conv_w"], w["conv_b"], w["w_down"], w["g_ffn_post"])


def kernel(x_prompt, x_sample, mem_prompt, mem_sample, g_mix_pre, g_mix_post, g_mem, w_in,
           decay_fwd, decay_bwd, ret_gn, w_ret_out, pool_w, pool_scale, w_pool_out,
           w_mem_kv, w_mem_out, w_o, g_ffn_pre, g_ffn_post, w_up, conv_w, conv_b, w_down):
    depth = w_in.shape[0]
    for l in range(depth):
        w = {
            "g_mix_pre": g_mix_pre[l][None], "g_mix_post": g_mix_post[l][None], "g_mem": g_mem[l][None],
            "w_in": w_in[l].astype(BF16),
            "decay": jnp.stack([decay_fwd[l], decay_bwd[l]]).astype(F32),
            "ret_gn": ret_gn[l][None],
            "w_ret_out": w_ret_out[l].astype(BF16),
            "w_pool_fold": _pool_fold(pool_w[l], pool_scale[l][None], w_pool_out[l]),
            "w_mem_kv": w_mem_kv[l].astype(BF16), "w_mem_out": w_mem_out[l].astype(BF16),
            "w_o": w_o[l].astype(BF16),
            "g_ffn_pre": g_ffn_pre[l][None], "g_ffn_post": g_ffn_post[l][None],
            "w_up": w_up[l].astype(BF16), "conv_w": conv_w[l], "conv_b": conv_b[l][None],
            "w_down": w_down[l].astype(BF16),
        }
        x_prompt = _layer(x_prompt, mem_prompt, w)
        x_sample = _layer(x_sample, mem_sample, w)
    return (x_prompt, x_sample)
```

```python
import functools

import jax
import jax.numpy as jnp
from jax import lax
from jax.experimental import pallas as pl
from jax.experimental.pallas import tpu as pltpu

F32 = jnp.float32
BF16 = jnp.bfloat16

D_MODEL = 1024
N_HEADS = 4
HEAD_W = 256
HALF_W = HEAD_W // 2
RET_CHUNK = 512
ROPE_BASE = 10000.0
POOL_WINDOWS = (2, 4, 8, 16)
MEM_TOKENS = 256
N_BRANCH = 3
D_FF = 2816
EPS = 1e-6
QK_SCALE = HEAD_W ** -0.5

VMEM_LIMIT_BYTES = 58 * 1024 * 1024
LANES = 128
BF16_SUBLANES = 16
F32_SUBLANES = 8

ROW_TILE = 512
FF_CHUNK = 256
FF_BUFS = 11
STATE_GROUP = 4


def _rms(x, g):
    ms = jnp.mean(x * x, axis=-1, keepdims=True)
    return x * lax.rsqrt(ms + EPS) * g


def _sigmoid(x):
    return 1.0 / (1.0 + jnp.exp(-x))


def _log_sigmoid(x):
    return -(jnp.maximum(-x, 0.0) + jnp.log1p(jnp.exp(-jnp.abs(x))))


def _head_cols(hd):
    return slice(hd * HEAD_W, (hd + 1) * HEAD_W)


def _const_spec(shape):
    zeros = (0,) * len(shape)
    return pl.BlockSpec(shape, lambda *_: zeros, pipeline_mode=pl.Buffered(1))


def _mem_kv_kernel(m_ref, g_ref, w_kv_ref, w_out_ref, k_ref, vw_ref):
    h = _rms(m_ref[0], g_ref[...]).astype(BF16)
    kv = jnp.dot(h, w_kv_ref[...], preferred_element_type=F32)
    k_ref[0] = kv[:, :D_MODEL].astype(BF16)
    for hd in range(N_HEADS):
        cols = _head_cols(hd)
        vh = kv[:, D_MODEL + hd * HEAD_W:D_MODEL + (hd + 1) * HEAD_W].astype(BF16)
        vw_ref[0, cols, :] = jnp.dot(vh, w_out_ref[cols, :], preferred_element_type=F32).astype(BF16)


def _mem_kv(mem, g_mem, w_kv, w_mem_out):
    batch = mem.shape[0]
    return pl.pallas_call(
        _mem_kv_kernel,
        out_shape=(jax.ShapeDtypeStruct((batch, MEM_TOKENS, D_MODEL), BF16),
                   jax.ShapeDtypeStruct((batch, N_HEADS * MEM_TOKENS, D_MODEL), BF16)),
        grid=(batch,),
        in_specs=[
            pl.BlockSpec((1, MEM_TOKENS, D_MODEL), lambda i: (i, 0, 0)),
            _const_spec((1, D_MODEL)),
            _const_spec(w_kv.shape),
            _const_spec(w_mem_out.shape),
        ],
        out_specs=(pl.BlockSpec((1, MEM_TOKENS, D_MODEL), lambda i: (i, 0, 0)),
                   pl.BlockSpec((1, N_HEADS * MEM_TOKENS, D_MODEL), lambda i: (i, 0, 0))),
        compiler_params=pltpu.CompilerParams(
            dimension_semantics=("parallel",), vmem_limit_bytes=VMEM_LIMIT_BYTES),
        name="mem_kv",
    )(mem, g_mem, w_kv, w_mem_out)


def _pool_fold_kernel(pw_ref, scale_ref, w_out_ref, o_ref):
    a = pw_ref[0] * scale_ref[...]
    o_ref[...] = jnp.dot(a, w_out_ref[...], preferred_element_type=F32,
                         precision=lax.Precision.HIGHEST).astype(BF16)


def _pool_fold(pool_w, pool_scale, w_pool_out):
    return pl.pallas_call(
        _pool_fold_kernel,
        out_shape=jax.ShapeDtypeStruct((D_MODEL, D_MODEL), BF16),
        grid=(N_HEADS,),
        in_specs=[
            pl.BlockSpec((1, HEAD_W, HEAD_W), lambda g: (g, 0, 0)),
            pl.BlockSpec((1, HEAD_W), lambda g: (0, g)),
            pl.BlockSpec((HEAD_W, D_MODEL), lambda g: (g, 0)),
        ],
        out_specs=pl.BlockSpec((HEAD_W, D_MODEL), lambda g: (g, 0)),
        compiler_params=pltpu.CompilerParams(
            dimension_semantics=("parallel",), vmem_limit_bytes=VMEM_LIMIT_BYTES),
        name="pool_fold",
    )(pool_w, pool_scale, w_pool_out)


def _in_proj_kernel(dec_ref, x_ref, g_ref, w_ref, cos_ref, sin_ref,
                    q_ref, k_ref, v_ref, sg_ref, p_ref, mq_ref, gt_ref, kv_ref, kdec_ref):
    c = RET_CHUNK

    @pl.when(pl.program_id(0) == 0)
    def _init_key_decay():
        idx = lax.broadcasted_iota(jnp.int32, (F32_SUBLANES, c), 1).astype(F32)
        for hd in range(N_HEADS):
            kdec_ref[0, hd] = jnp.exp(
                _log_sigmoid(jnp.full((F32_SUBLANES, c), dec_ref[0, hd], F32)) * (c - 1.0 - idx))
            kdec_ref[1, hd] = jnp.exp(
                _log_sigmoid(jnp.full((F32_SUBLANES, c), dec_ref[1, hd], F32)) * idx)

    xx = x_ref[...]
    h = (xx * g_ref[...]).astype(BF16)
    rinv = lax.rsqrt(jnp.mean(xx * xx, axis=-1, keepdims=True) + EPS)
    cos = cos_ref[...]
    sin = sin_ref[...]

    def seg(i):
        return jnp.dot(h, w_ref[:, i * D_MODEL:(i + 1) * D_MODEL], preferred_element_type=F32) * rinv

    def rotary(z, hd, scale):
        lo = hd * HEAD_W
        x1 = z[:, lo:lo + HALF_W]
        x2 = z[:, lo + HALF_W:lo + HEAD_W]
        return jnp.concatenate([(x1 * cos - x2 * sin) * scale, (x1 * sin + x2 * cos) * scale], axis=1)

    for b in range(N_BRANCH):
        gt_ref[:, b * D_MODEL:(b + 1) * D_MODEL] = _sigmoid(seg(6 + b)).astype(BF16)
    rg = seg(3)
    sg_ref[...] = (rg * _sigmoid(rg)).astype(BF16)
    v_ref[...] = seg(2).astype(BF16)
    zk = seg(1)
    kts = []
    for hd in range(N_HEADS):
        kh = rotary(zk, hd, QK_SCALE)
        k_ref[:, _head_cols(hd)] = kh.astype(BF16)
        kts.append(kh.T)
    zq = seg(0)
    for hd in range(N_HEADS):
        q_ref[:, _head_cols(hd)] = rotary(zq, hd, 1.0).astype(BF16)
    p_ref[...] = seg(4).astype(BF16)
    mq_ref[...] = seg(5).astype(BF16)
    for hd in range(N_HEADS):
        cols = _head_cols(hd)
        for dr in range(2):
            kd = (kts[hd] * kdec_ref[dr, hd, 0:1, :]).astype(BF16)
            kv_ref[0, dr, cols, :] = jnp.dot(kd, v_ref[:, cols], preferred_element_type=F32).astype(BF16)


def _in_proj(dec, x2d, g, w_in, cos, sin, seq):
    rows = x2d.shape[0]
    tm = ROW_TILE
    assert tm == RET_CHUNK
    tiles_per_seq = seq // tm
    n_tiles = rows // tm
    row_spec = pl.BlockSpec((tm, D_MODEL), lambda i: (i, 0))
    tab_spec = pl.BlockSpec((tm, HALF_W), lambda i: (i % tiles_per_seq, 0))
    out1 = jax.ShapeDtypeStruct((rows, D_MODEL), BF16)
    return pl.pallas_call(
        _in_proj_kernel,
        out_shape=(out1,) * 6 + (jax.ShapeDtypeStruct((rows, N_BRANCH * D_MODEL), BF16),
                                 jax.ShapeDtypeStruct((n_tiles, 2, D_MODEL, HEAD_W), BF16)),
        grid=(n_tiles,),
        in_specs=[pl.BlockSpec(memory_space=pltpu.SMEM),
                  row_spec, _const_spec((1, D_MODEL)), _const_spec(w_in.shape), tab_spec, tab_spec],
        out_specs=(row_spec,) * 6 + (pl.BlockSpec((tm, N_BRANCH * D_MODEL), lambda i: (i, 0)),
                                     pl.BlockSpec((1, 2, D_MODEL, HEAD_W), lambda i: (i, 0, 0, 0))),
        scratch_shapes=[pltpu.VMEM((2, N_HEADS, F32_SUBLANES, RET_CHUNK), F32)],
        compiler_params=pltpu.CompilerParams(
            dimension_semantics=("arbitrary",), vmem_limit_bytes=VMEM_LIMIT_BYTES),
        name="in_proj",
    )(dec, x2d, g, w_in, cos, sin)


def _ret_state_kernel(dec_ref, kvb_ref, ob_ref, sb_ref, cdec_ref):
    group = kvb_ref.shape[0]

    @pl.when(pl.program_id(1) == 0)
    def _init_sequence():
        sb_ref[...] = jnp.zeros_like(sb_ref)
        for hd in range(N_HEADS):
            cdec_ref[hd] = jnp.exp(
                _log_sigmoid(jnp.full((F32_SUBLANES, HEAD_W), dec_ref[1, hd], F32)) * RET_CHUNK)

    for j in reversed(range(group)):
        for hd in range(N_HEADS):
            cols = _head_cols(hd)
            state = sb_ref[cols, :]
            ob_ref[j, cols, :] = state.astype(BF16)
            sb_ref[cols, :] = state * cdec_ref[hd, 0:1, :] + kvb_ref[j, 0, cols, :].astype(F32)


def _ret_state(dec, kv, batch):
    n_chunks = kv.shape[0] // batch
    group = min(STATE_GROUP, n_chunks)
    n_groups = n_chunks // group
    return pl.pallas_call(
        _ret_state_kernel,
        out_shape=jax.ShapeDtypeStruct((batch * n_chunks, D_MODEL, HEAD_W), BF16),
        grid=(batch, n_groups),
        in_specs=[
            pl.BlockSpec(memory_space=pltpu.SMEM),
            pl.BlockSpec((group, 1, D_MODEL, HEAD_W),
                         lambda b, t: (b * n_groups + n_groups - 1 - t, 1, 0, 0)),
        ],
        out_specs=pl.BlockSpec((group, D_MODEL, HEAD_W),
                               lambda b, t: (b * n_groups + n_groups - 1 - t, 0, 0)),
        scratch_shapes=[
            pltpu.VMEM((D_MODEL, HEAD_W), F32),
            pltpu.VMEM((N_HEADS, F32_SUBLANES, HEAD_W), F32),
        ],
        compiler_params=pltpu.CompilerParams(
            dimension_semantics=("arbitrary", "arbitrary"), vmem_limit_bytes=VMEM_LIMIT_BYTES),
        name="ret_state",
    )(dec, kv)


def _mix_out_kernel(dec_ref, x_ref, q_ref, k_ref, v_ref, sg_ref, kvf_ref, stb_ref, gn_ref,
                    p_ref, pprev_ref, pnext_ref, mq_ref, gt_ref, km_ref, vw_ref,
                    w_ret_ref, w_pool_ref, w_o_ref, g_ref, o_ref,
                    pext_ref, psum_ref, dmask_ref, qdec_ref, sf_ref, cdec_ref, *, seq):
    tm = ROW_TILE
    c = RET_CHUNK
    halo = BF16_SUBLANES
    t = pl.program_id(1)
    has_prev = (t > 0).astype(F32)
    has_next = (t < pl.num_programs(1) - 1).astype(F32)

    @pl.when(jnp.logical_and(pl.program_id(0) == 0, t == 0))
    def _init_decay():
        diff = (lax.broadcasted_iota(jnp.int32, (c, c), 0)
                - lax.broadcasted_iota(jnp.int32, (c, c), 1)).astype(F32)
        idx = lax.broadcasted_iota(jnp.int32, (c, LANES), 0).astype(F32)
        for hd in range(N_HEADS):
            lg_f = _log_sigmoid(jnp.full((c, c), dec_ref[0, hd], F32))
            lg_b = _log_sigmoid(jnp.full((c, c), dec_ref[1, hd], F32))
            dmask_ref[hd] = jnp.where(diff >= 0, jnp.exp(lg_f * jnp.maximum(diff, 0.0)),
                                      jnp.exp(lg_b * jnp.maximum(-diff, 0.0))).astype(BF16)
            qdec_ref[0, hd] = jnp.exp(_log_sigmoid(jnp.full((c, LANES), dec_ref[0, hd], F32)) * (idx + 1.0))
            qdec_ref[1, hd] = jnp.exp(_log_sigmoid(jnp.full((c, LANES), dec_ref[1, hd], F32)) * (c - idx))
            cdec_ref[hd] = jnp.exp(
                _log_sigmoid(jnp.full((F32_SUBLANES, HEAD_W), dec_ref[0, hd], F32)) * c)

    @pl.when(t == 0)
    def _init_state():
        sf_ref[...] = jnp.zeros_like(sf_ref)

    def cross(q, state, dr, hd):
        y = jnp.dot(q, state, preferred_element_type=F32)
        qd = qdec_ref[dr, hd]
        return jnp.concatenate([y[:, :LANES] * qd, y[:, LANES:] * qd], axis=1)

    yr = []
    for hd in range(N_HEADS):
        cols = _head_cols(hd)
        q = q_ref[0, :, cols]
        state_f = sf_ref[cols, :]
        s = lax.dot_general(q, k_ref[0, :, cols], (((1,), (1,)), ((), ())),
                            preferred_element_type=F32)
        pm = (s * dmask_ref[hd].astype(F32)).astype(BF16)
        y = (jnp.dot(pm, v_ref[0, :, cols], preferred_element_type=F32)
             + cross(q, state_f.astype(BF16), 0, hd) + cross(q, stb_ref[0, cols, :], 1, hd))
        sf_ref[cols, :] = state_f * cdec_ref[hd, 0:1, :] + kvf_ref[0, 0, cols, :].astype(F32)
        mu = jnp.sum(y[:, :LANES] + y[:, LANES:], axis=-1, keepdims=True) * (1.0 / HEAD_W)
        yc = y - mu
        sq = yc * yc
        var = jnp.sum(sq[:, :LANES] + sq[:, LANES:], axis=-1, keepdims=True) * (1.0 / HEAD_W)
        yn = yc * lax.rsqrt(var + EPS) * gn_ref[:, cols]
        yr.append((sg_ref[0, :, cols].astype(F32) * yn).astype(BF16))
    merged = gt_ref[0, :, 0:D_MODEL].astype(F32) * jnp.dot(
        jnp.concatenate(yr, axis=1), w_ret_ref[...], preferred_element_type=F32)

    pprev = pprev_ref[0].astype(F32) * has_prev
    pmain = p_ref[0].astype(F32)
    pnext = pnext_ref[0].astype(F32) * has_next
    for ct in range(D_MODEL // LANES):
        lanes = slice(ct * LANES, (ct + 1) * LANES)
        pext_ref[ct, 0:halo, :] = pprev[:, lanes]
        pext_ref[ct, halo:halo + tm, :] = pmain[:, lanes]
        pext_ref[ct, halo + tm:, :] = pnext[:, lanes]
    pos = t * tm + lax.broadcasted_iota(jnp.int32, (tm, LANES), 0)
    d = []
    for gi, w in enumerate(POOL_WINDOWS):
        cnt = (jnp.minimum(pos + w // 2, seq) - jnp.maximum(pos - w // 2, 0)).astype(F32)
        inv_cnt = 1.0 / cnt
        for ct in range(gi * HEAD_W // LANES, (gi + 1) * HEAD_W // LANES):
            start = halo - w // 2
            rows = tm + w - 2
            acc = pext_ref[ct, start:start + rows, :] + pext_ref[ct, start + 1:start + 1 + rows, :]
            span = 2
            while span < w:
                psum_ref[ct % 2, 0:rows, :] = acc
                rows -= span
                acc = psum_ref[ct % 2, 0:rows, :] + psum_ref[ct % 2, span:span + rows, :]
                span *= 2
            d.append(acc * inv_cnt - pext_ref[ct, halo:halo + tm, :])
    d = jnp.concatenate(d, axis=1).astype(BF16)
    y_pool = jnp.dot(d, w_pool_ref[...], preferred_element_type=F32)
    merged = merged + gt_ref[0, :, D_MODEL:2 * D_MODEL].astype(F32) * y_pool

    probs = []
    for hd in range(N_HEADS):
        cols = _head_cols(hd)
        s = lax.dot_general(mq_ref[0, :, cols], km_ref[0, :, cols], (((1,), (1,)), ((), ())),
                            preferred_element_type=F32)
        e = jnp.exp((s - jnp.max(s, axis=-1, keepdims=True)) * QK_SCALE)
        probs.append((e * (1.0 / jnp.sum(e, axis=-1, keepdims=True))).astype(BF16))
    y_mem = jnp.dot(jnp.concatenate(probs, axis=1), vw_ref[0], preferred_element_type=F32)
    merged = merged + gt_ref[0, :, 2 * D_MODEL:].astype(F32) * y_mem

    y = jnp.dot(merged.astype(BF16), w_o_ref[...], preferred_element_type=F32)
    o_ref[0] = x_ref[0] + _rms(y, g_ref[...])


def _mix_out(dec, x, q, k, v, sg, kv, st_b, ret_gn, p, mq, gates, k_mem, vw_mem,
             w_ret, w_pool, w_o, g_post):
    batch, seq, _ = x.shape
    tm = ROW_TILE
    assert tm == RET_CHUNK
    halo = BF16_SUBLANES
    n_tiles = seq // tm
    halo_per_tile = tm // halo
    n_halo_blocks = seq // halo
    row_spec = pl.BlockSpec((1, tm, D_MODEL), lambda b, t: (b, t, 0))
    prev_spec = pl.BlockSpec((1, halo, D_MODEL),
                             lambda b, t: (b, jnp.maximum(t * halo_per_tile - 1, 0), 0))
    next_spec = pl.BlockSpec((1, halo, D_MODEL),
                             lambda b, t: (b, jnp.minimum((t + 1) * halo_per_tile, n_halo_blocks - 1), 0))
    state_spec = pl.BlockSpec((1, D_MODEL, HEAD_W), lambda b, t: (b * n_tiles + t, 0, 0))
    return pl.pallas_call(
        functools.partial(_mix_out_kernel, seq=seq),
        out_shape=jax.ShapeDtypeStruct(x.shape, F32),
        grid=(batch, n_tiles),
        in_specs=[
            pl.BlockSpec(memory_space=pltpu.SMEM),
            row_spec, row_spec, row_spec, row_spec, row_spec,
            pl.BlockSpec((1, 1, D_MODEL, HEAD_W), lambda b, t: (b * n_tiles + t, 0, 0, 0)), state_spec,
            _const_spec(ret_gn.shape),
            row_spec, prev_spec, next_spec, row_spec,
            pl.BlockSpec((1, tm, N_BRANCH * D_MODEL), lambda b, t: (b, t, 0)),
            pl.BlockSpec((1, MEM_TOKENS, D_MODEL), lambda b, t: (b, 0, 0)),
            pl.BlockSpec((1, N_HEADS * MEM_TOKENS, D_MODEL), lambda b, t: (b, 0, 0)),
            _const_spec(w_ret.shape), _const_spec(w_pool.shape), _const_spec(w_o.shape),
            _const_spec(g_post.shape),
        ],
        out_specs=row_spec,
        scratch_shapes=[
            pltpu.VMEM((D_MODEL // LANES, tm + 2 * halo, LANES), F32),
            pltpu.VMEM((2, tm + 2 * halo, LANES), F32),
            pltpu.VMEM((N_HEADS, RET_CHUNK, RET_CHUNK), BF16),
            pltpu.VMEM((2, N_HEADS, RET_CHUNK, LANES), F32),
            pltpu.VMEM((D_MODEL, HEAD_W), F32),
            pltpu.VMEM((N_HEADS, F32_SUBLANES, HEAD_W), F32),
        ],
        compiler_params=pltpu.CompilerParams(
            dimension_semantics=("arbitrary", "arbitrary"), vmem_limit_bytes=VMEM_LIMIT_BYTES),
        name="mix_out",
    )(dec, x, q, k, v, sg, kv, st_b, ret_gn, p, p, p, mq, gates, k_mem, vw_mem,
      w_ret, w_pool, w_o, g_post)


def _gelu_tanh(x):
    return 0.5 * x * (1.0 + jnp.tanh(0.7978845608028654 * (x + 0.044715 * (x * x * x))))


def _ffn_kernel(x_ref, xprev_ref, xnext_ref, g_pre_ref, w_up_ref, conv_w_ref, conv_b_ref,
                w_down_ref, g_post_ref, o_ref, u_ref):
    tm = ROW_TILE
    halo = F32_SUBLANES
    t = pl.program_id(1)
    has_prev = (t > 0).astype(F32)
    has_next = (t < pl.num_programs(1) - 1).astype(F32)
    x = x_ref[0]
    xext = jnp.concatenate([xprev_ref[0] * has_prev, x, xnext_ref[0] * has_next], axis=0)
    hext = _rms(xext, g_pre_ref[...]).astype(BF16)

    n_chunks = D_FF // FF_CHUNK
    lane_tiles = FF_CHUNK // LANES

    def up(j):
        for half in range(2):
            col0 = half * D_FF + j * FF_CHUNK
            u = jnp.dot(hext, w_up_ref[:, col0:col0 + FF_CHUNK], preferred_element_type=F32)
            for ct in range(lane_tiles):
                u_ref[j % FF_BUFS, half, ct] = u[:, ct * LANES:(ct + 1) * LANES]

    def conv(j, half):
        outs = []
        for ct in range(lane_tiles):
            u = u_ref.at[j % FF_BUFS, half, ct]
            col0 = half * D_FF + j * FF_CHUNK + ct * LANES
            cols = slice(col0, col0 + LANES)
            outs.append(u[halo - 1:halo - 1 + tm, :] * conv_w_ref[0:1, cols]
                        + u[halo:halo + tm, :] * conv_w_ref[1:2, cols]
                        + u[halo + 1:halo + 1 + tm, :] * conv_w_ref[2:3, cols]
                        + conv_b_ref[:, cols])
        return jnp.concatenate(outs, axis=1)

    acc = None
    for j in range(min(FF_BUFS - 1, n_chunks)):
        up(j)
    for j in range(n_chunks):
        if j + FF_BUFS - 1 < n_chunks:
            up(j + FF_BUFS - 1)
        a = (_gelu_tanh(conv(j, 0)) * conv(j, 1)).astype(BF16)
        part = jnp.dot(a, w_down_ref[j * FF_CHUNK:(j + 1) * FF_CHUNK, :], preferred_element_type=F32)
        acc = part if acc is None else acc + part
    o_ref[0] = x + _rms(acc, g_post_ref[...])


def _ffn(x, g_pre, w_up, conv_w, conv_b, w_down, g_post):
    batch, seq, _ = x.shape
    tm = ROW_TILE
    halo = F32_SUBLANES
    n_tiles = seq // tm
    halo_per_tile = tm // halo
    n_halo_blocks = seq // halo
    row_spec = pl.BlockSpec((1, tm, D_MODEL), lambda b, t: (b, t, 0))
    prev_spec = pl.BlockSpec((1, halo, D_MODEL),
                             lambda b, t: (b, jnp.maximum(t * halo_per_tile - 1, 0), 0))
    next_spec = pl.BlockSpec((1, halo, D_MODEL),
                             lambda b, t: (b, jnp.minimum((t + 1) * halo_per_tile, n_halo_blocks - 1), 0))
    return pl.pallas_call(
        _ffn_kernel,
        out_shape=jax.ShapeDtypeStruct(x.shape, F32),
        grid=(batch, n_tiles),
        in_specs=[
            row_spec, prev_spec, next_spec,
            _const_spec(g_pre.shape), _const_spec(w_up.shape), _const_spec(conv_w.shape),
            _const_spec(conv_b.shape), _const_spec(w_down.shape), _const_spec(g_post.shape),
        ],
        out_specs=row_spec,
        scratch_shapes=[pltpu.VMEM((FF_BUFS, 2, FF_CHUNK // LANES, tm + 2 * halo, LANES), F32)],
        compiler_params=pltpu.CompilerParams(
            dimension_semantics=("parallel", "parallel"), vmem_limit_bytes=VMEM_LIMIT_BYTES),
        name="ffn",
    )(x, x, x, g_pre, w_up, conv_w, conv_b, w_down, g_post)


def _rope_tables(seq):
    inv = ROPE_BASE ** (-jnp.arange(HALF_W, dtype=F32) / HALF_W)
    ang = jnp.arange(seq, dtype=F32)[:, None] * inv[None, :]
    return jnp.cos(ang), jnp.sin(ang)


def _layer(x, mem, w, cos, sin):
    batch, seq, _ = x.shape
    k_mem, vw_mem = _mem_kv(mem, w["g_mem"], w["w_mem_kv"], w["w_mem_out"])
    q, k, v, sg, p, mq, gates, kv = _in_proj(
        w["decay"], x.reshape(batch * seq, D_MODEL), w["g_mix_pre"], w["w_in"], cos, sin, seq)
    shp = (batch, seq, D_MODEL)
    q, k, v, sg = (a.reshape(shp) for a in (q, k, v, sg))
    st_b = _ret_state(w["decay"], kv, batch)
    x1 = _mix_out(w["decay"], x, q, k, v, sg, kv, st_b, w["ret_gn"], p.reshape(shp), mq.reshape(shp),
                  gates.reshape(batch, seq, N_BRANCH * D_MODEL), k_mem, vw_mem,
                  w["w_ret_out"], w["w_pool_fold"], w["w_o"], w["g_mix_post"])
    return _ffn(x1, w["g_ffn_pre"], w["w_up"], w["conv_w"], w["conv_b"], w["w_down"], w["g_ffn_post"])


def kernel(x_prompt, x_sample, mem_prompt, mem_sample, g_mix_pre, g_mix_post, g_mem, w_in,
           decay_fwd, decay_bwd, ret_gn, w_ret_out, pool_w, pool_scale, w_pool_out,
           w_mem_kv, w_mem_out, w_o, g_ffn_pre, g_ffn_post, w_up, conv_w, conv_b, w_down):
    depth = w_in.shape[0]
    cos, sin = _rope_tables(max(x_prompt.shape[1], x_sample.shape[1]))
    for l in range(depth):
        w = {
            "g_mix_pre": g_mix_pre[l][None], "g_mix_post": g_mix_post[l][None], "g_mem": g_mem[l][None],
            "w_in": w_in[l].astype(BF16),
            "decay": jnp.stack([decay_fwd[l], decay_bwd[l]]).astype(F32),
            "ret_gn": ret_gn[l][None],
            "w_ret_out": w_ret_out[l].astype(BF16),
            "w_pool_fold": _pool_fold(pool_w[l], pool_scale[l][None], w_pool_out[l]),
            "w_mem_kv": w_mem_kv[l].astype(BF16), "w_mem_out": w_mem_out[l].astype(BF16),
            "w_o": w_o[l].astype(BF16),
            "g_ffn_pre": g_ffn_pre[l][None], "g_ffn_post": g_ffn_post[l][None],
            "w_up": w_up[l].astype(BF16), "conv_w": conv_w[l], "conv_b": conv_b[l][None],
            "w_down": w_down[l].astype(BF16),
        }
        x_prompt = _layer(x_prompt, mem_prompt, w, cos, sin)
        x_sample = _layer(x_sample, mem_sample, w, cos, sin)
    return (x_prompt, x_sample)
```
